```python
import jax, jax.numpy as jnp
from jax import lax
import numpy as np

D_MODEL = 2048
BATCH = 8
SEQ = 4096
DEPTH = 1
DEC_BATCH = 16
DEC_SEQ = 64
PAST_LEN = 2048

CHUNK = 64
N_PAST_CHUNKS = 8
ATTN_WINDOW = N_PAST_CHUNKS * CHUNK
BAND = ATTN_WINDOW + CHUNK
D_ATTN = D_MODEL // 2
HEAD_DIM = 128
N_HEADS_A = D_ATTN // HEAD_DIM
D_GMLP = D_MODEL - D_ATTN
N_GROUPS_B = 8
GROUP_DIM_B = D_GMLP // N_GROUPS_B
GMLP_CHUNK = 128
REL_CLIP = 128
D_FF = ((8 * D_MODEL // 3 + 127) // 128) * 128
CONV_W = 3
D_IN = 3 * D_ATTN + 2 * D_GMLP
EPS = 1e-6
NEG_INF = -1e30

kernel_name = "hymba_chunk_attn_gmlp_convffn_step"


def rms_norm(x, g):
    xf = x.astype(jnp.float32)
    y = xf * lax.rsqrt(jnp.mean(xf * xf, axis=-1, keepdims=True) + EPS)
    return (y * g.astype(jnp.float32)).astype(x.dtype)


def layer_norm(x, g, b):
    xf = x.astype(jnp.float32)
    mu = jnp.mean(xf, axis=-1, keepdims=True)
    var = jnp.mean(jnp.square(xf - mu), axis=-1, keepdims=True)
    y = (xf - mu) * lax.rsqrt(var + EPS)
    return (y * g.astype(jnp.float32) + b.astype(jnp.float32)).astype(x.dtype)


def rel_bias(table, q_pos, k_pos):
    rel = jnp.clip(q_pos[:, None] - k_pos[None, :], -REL_CLIP, REL_CLIP) + REL_CLIP
    return table[:, rel].astype(jnp.float32)


def attend(q, k, v, bias, mask):
    s = jnp.einsum('bqhd,bkhd->bhqk', q, k).astype(jnp.float32) * (HEAD_DIM ** -0.5) + bias[None]
    if mask is not None:
        s = jnp.where(mask, s, NEG_INF)
    p = jax.nn.softmax(s, axis=-1).astype(v.dtype)
    return jnp.einsum('bhqk,bkhd->bqhd', p, v)


def mix_inputs(n, w_in, q_g, k_g, ln_g, ln_b):
    B, T, _ = n.shape
    z = n @ w_in
    q, k, va, u, vb = jnp.split(z, [D_ATTN, 2 * D_ATTN, 3 * D_ATTN, 3 * D_ATTN + D_GMLP], axis=-1)
    q = rms_norm(q.reshape(B, T, N_HEADS_A, HEAD_DIM), q_g)
    k = rms_norm(k.reshape(B, T, N_HEADS_A, HEAD_DIM), k_g)
    va = va.reshape(B, T, N_HEADS_A, HEAD_DIM)
    u = jax.nn.gelu(u, approximate=False)
    vg = jax.nn.gelu(vb, approximate=False).reshape(B, T, N_GROUPS_B, GROUP_DIM_B)
    vn = layer_norm(vg, ln_g, ln_b)
    return q, k, va, u, vn


def spatial_gate(u, vn, w_s, b_s):
    L = vn.shape[2]
    tri = jnp.tril(jnp.ones((L, L), dtype=bool))
    ws = jnp.where(tri[None], w_s[:, :L, :L], 0.0).astype(vn.dtype)
    vs = jnp.einsum('gts,bnsgc->bntgc', ws, vn) + b_s[:, :L].T[None, None, :, :, None]
    return u * vs.reshape(u.shape)


def causal_dwconv(prev, a, w, b):
    T = a.shape[1]
    ap = jnp.concatenate([prev, a], axis=1)
    out = b + sum(w[i] * ap[:, i:i + T] for i in range(CONV_W))
    return out, ap[:, ap.shape[1] - (CONV_W - 1):]


def conv_ffn(h, prev, g, w_up, cw, cb, w_down):
    n = rms_norm(h, g)
    a, gv = jnp.split(n @ w_up, 2, axis=-1)
    ac, new_prev = causal_dwconv(prev, a, cw, cb)
    return h + (jax.nn.silu(ac) * gv) @ w_down, new_prev


def prompt_layer(x, nmg, w_in, qg, kg, tab, lng, lnb, ws, bs, w_out, nfg, w_up, cw, cb, w_down):
    B, S, _ = x.shape
    q, k, va, u, vn = mix_inputs(rms_norm(x, nmg), w_in, qg, kg, lng, lnb)
    nc = S // CHUNK
    pad = ((0, 0), (ATTN_WINDOW, 0), (0, 0), (0, 0))
    kp = jnp.pad(k, pad)
    vp = jnp.pad(va, pad)
    li = jnp.arange(CHUNK)
    lj = jnp.arange(BAND)
    bias = rel_bias(tab, li + ATTN_WINDOW, lj)
    qc = jnp.moveaxis(q.reshape(B, nc, CHUNK, N_HEADS_A, HEAD_DIM), 1, 0)

    def one_chunk(args):
        qb, c = args
        kb = lax.dynamic_slice_in_dim(kp, c * CHUNK, BAND, axis=1)
        vb = lax.dynamic_slice_in_dim(vp, c * CHUNK, BAND, axis=1)
        valid = (c * CHUNK - ATTN_WINDOW + lj) >= 0
        return attend(qb, kb, vb, bias, valid[None, None, None, :])

    oa = lax.map(one_chunk, (qc, jnp.arange(nc)))
    oa = jnp.moveaxis(oa, 0, 1).reshape(B, S, D_ATTN)
    ob = spatial_gate(u, vn.reshape(B, S // GMLP_CHUNK, GMLP_CHUNK, N_GROUPS_B, GROUP_DIM_B), ws, bs)
    h = x + jnp.concatenate([oa, ob], axis=-1) @ w_out
    prev0 = jnp.zeros((B, CONV_W - 1, D_FF), dtype=x.dtype)
    y, conv_state = conv_ffn(h, prev0, nfg, w_up, cw, cb, w_down)
    keep = min(ATTN_WINDOW, S)
    return y, k[:, S - keep:], va[:, S - keep:], conv_state


def sample_layer(x, ck, cv, cst, nmg, w_in, qg, kg, tab, lng, lnb, ws, bs, w_out, nfg, w_up, cw, cb, w_down):
    B, T, _ = x.shape
    W = ck.shape[1]
    q, k, va, u, vn = mix_inputs(rms_norm(x, nmg), w_in, qg, kg, lng, lnb)
    q_pos = PAST_LEN + jnp.arange(T)
    k_pos = jnp.concatenate([PAST_LEN - W + jnp.arange(W), q_pos])
    k_all = jnp.concatenate([ck, k], axis=1)
    v_all = jnp.concatenate([cv, va], axis=1)
    oa = attend(q, k_all, v_all, rel_bias(tab, q_pos, k_pos), None).reshape(B, T, D_ATTN)
    ob = spatial_gate(u, vn.reshape(B, 1, T, N_GROUPS_B, GROUP_DIM_B), ws, bs)
    h = x + jnp.concatenate([oa, ob], axis=-1) @ w_out
    y, conv_state = conv_ffn(h, cst, nfg, w_up, cw, cb, w_down)
    return y, k, va, vn.reshape(B, T, D_GMLP), conv_state


def setup_inputs(seed: int = 0) -> dict:
    key = jax.random.key(seed)
    ks = jax.random.split(key, 24)
    f = jnp.float32
    nrm = lambda i, shape, s: (jax.random.normal(ks[i], shape, f) * s)
    w_cache = min(ATTN_WINDOW, PAST_LEN)
    return {
        "x_prompt": nrm(0, (BATCH, SEQ, D_MODEL), 1.0),
        "x_sample": nrm(1, (DEC_BATCH, DEC_SEQ, D_MODEL), 1.0),
        "cache_attn_k": nrm(2, (DEPTH, DEC_BATCH, w_cache, N_HEADS_A, HEAD_DIM), 1.0),
        "cache_attn_v": nrm(3, (DEPTH, DEC_BATCH, w_cache, N_HEADS_A, HEAD_DIM), 1.0),
        "state_ffn_conv": nrm(4, (DEPTH, DEC_BATCH, CONV_W - 1, D_FF), 1.0),
        "norm_mix_g": 1.0 + nrm(5, (DEPTH, D_MODEL), 0.02),
        "w_in": nrm(6, (DEPTH, D_MODEL, D_IN), D_MODEL ** -0.5),
        "q_norm_g": 1.0 + nrm(7, (DEPTH, HEAD_DIM), 0.02),
        "k_norm_g": 1.0 + nrm(8, (DEPTH, HEAD_DIM), 0.02),
        "rel_bias_table": nrm(9, (DEPTH, N_HEADS_A, 2 * REL_CLIP + 1), 0.1),
        "gmlp_ln_g": 1.0 + nrm(10, (DEPTH, GROUP_DIM_B), 0.02),
        "gmlp_ln_b": nrm(11, (DEPTH, GROUP_DIM_B), 0.02),
        "gmlp_w_s": nrm(12, (DEPTH, N_GROUPS_B, GMLP_CHUNK, GMLP_CHUNK), GMLP_CHUNK ** -0.5),
        "gmlp_b_s": 1.0 + nrm(13, (DEPTH, N_GROUPS_B, GMLP_CHUNK), 0.02),
        "w_out": nrm(14, (DEPTH, D_MODEL, D_MODEL), D_MODEL ** -0.5),
        "norm_ffn_g": 1.0 + nrm(15, (DEPTH, D_MODEL), 0.02),
        "w_up": nrm(16, (DEPTH, D_MODEL, 2 * D_FF), D_MODEL ** -0.5),
        "ffn_conv_w": nrm(17, (DEPTH, CONV_W, D_FF), CONV_W ** -0.5),
        "ffn_conv_b": nrm(18, (DEPTH, D_FF), 0.02),
        "w_down": nrm(19, (DEPTH, D_FF, D_MODEL), D_FF ** -0.5),
    }


def reference(x_prompt, x_sample, cache_attn_k, cache_attn_v, state_ffn_conv,
              norm_mix_g, w_in, q_norm_g, k_norm_g, rel_bias_table, gmlp_ln_g, gmlp_ln_b,
              gmlp_w_s, gmlp_b_s, w_out, norm_ffn_g, w_up, ffn_conv_w, ffn_conv_b, w_down):
    xp, xs = x_prompt, x_sample
    pk, pv, pc, sk, sv, sg, sc = [], [], [], [], [], [], []
    for l in range(DEPTH):
        w = (norm_mix_g[l], w_in[l], q_norm_g[l], k_norm_g[l], rel_bias_table[l], gmlp_ln_g[l],
             gmlp_ln_b[l], gmlp_w_s[l], gmlp_b_s[l], w_out[l], norm_ffn_g[l], w_up[l],
             ffn_conv_w[l], ffn_conv_b[l], w_down[l])
        xp, k_p, v_p, c_p = prompt_layer(xp, *w)
        xs, k_s, v_s, g_s, c_s = sample_layer(xs, cache_attn_k[l], cache_attn_v[l], state_ffn_conv[l], *w)
        pk.append(k_p); pv.append(v_p); pc.append(c_p)
        sk.append(k_s); sv.append(v_s); sg.append(g_s); sc.append(c_s)
    return (xp, xs, jnp.stack(pk), jnp.stack(pv), jnp.stack(pc),
            jnp.stack(sk), jnp.stack(sv), jnp.stack(sg), jnp.stack(sc))
```

```python
import functools

import jax
import jax.numpy as jnp
from jax import lax
from jax.experimental import pallas as pl
from jax.experimental.pallas import tpu as pltpu

D_MODEL = 2048
CHUNK = 64
ATTN_WINDOW = 8 * CHUNK
BAND = ATTN_WINDOW + CHUNK
D_ATTN = D_MODEL // 2
HEAD_DIM = 128
N_HEADS = D_ATTN // HEAD_DIM
D_GMLP = D_MODEL - D_ATTN
N_GROUPS = 8
GROUP_DIM = D_GMLP // N_GROUPS
GMLP_CHUNK = 128
REL_CLIP = 128
D_FF = 5504
CONV_W = 3
D_IN = 3 * D_ATTN + 2 * D_GMLP
EPS = 1e-6
NEG_INF = -1e30

LANES = 128
SUBLANES = 8
VMEM_LIMIT_BYTES = 60 * 1024 * 1024

FF_TILE = 1408
D_FF_PAD = 4 * FF_TILE
ATTN_GROUP = 4
GQ = ATTN_GROUP * CHUNK
GK = GQ + ATTN_WINDOW

F32 = jnp.float32
BF16 = jnp.bfloat16


def _dot(a, b):
    return jnp.dot(a, b, preferred_element_type=F32)


def _gelu(x):
    return 0.5 * x * (1.0 + lax.erf(x * (0.5 ** 0.5)))


def _inproj_kernel(x_ref, nmg_ref, w_ref, qg_ref, kg_ref, lng_ref, lnb_ref,
                   z_ref, kf_ref, vf_ref, *rest, keep_every, emit_vn):
    i = pl.program_id(0)
    x = x_ref[...]
    ms = jnp.mean(x * x, axis=-1, keepdims=True)
    n = (x * lax.rsqrt(ms + EPS) * nmg_ref[...]).astype(BF16)
    keep = (i % keep_every) == (keep_every - 1)

    def head_rms(zh, g):
        r = lax.rsqrt(jnp.mean(zh * zh, axis=-1, keepdims=True) + EPS)
        return zh * r * g

    for seg in range(5):
        c0 = seg * D_ATTN
        zs = _dot(n, w_ref[:, c0:c0 + D_ATTN])
        for h in range(N_HEADS):
            lo = h * HEAD_DIM
            zh = zs[:, lo:lo + HEAD_DIM]
            if seg == 0:
                out = head_rms(zh, qg_ref[...])
            elif seg == 1:
                out = head_rms(zh, kg_ref[...])
            elif seg == 2:
                out = zh
            elif seg == 3:
                out = _gelu(zh)
            else:
                ge = _gelu(zh)
                mu = jnp.mean(ge, axis=-1, keepdims=True)
                d = ge - mu
                var = jnp.mean(d * d, axis=-1, keepdims=True)
                out = d * lax.rsqrt(var + EPS) * lng_ref[...] + lnb_ref[...]
            z_ref[:, c0 + lo:c0 + lo + HEAD_DIM] = out.astype(BF16)
            if seg == 1:
                @pl.when(keep)
                def _():
                    kf_ref[:, lo:lo + HEAD_DIM] = out
            elif seg == 2:
                @pl.when(keep)
                def _():
                    vf_ref[:, lo:lo + HEAD_DIM] = out
            elif seg == 4 and emit_vn:
                rest[0][:, lo:lo + HEAD_DIM] = out


def _inproj(x2d, nmg, w_in, qg, kg, lng, lnb, *, keep_every, emit_vn):
    m = x2d.shape[0]
    tm = 512
    nt = m // tm
    nkeep = nt // keep_every
    vec = lambda width: pl.BlockSpec((1, width), lambda i: (0, 0))
    keep_spec = pl.BlockSpec((tm, D_ATTN), lambda i: (i // keep_every, 0))
    out_specs = [pl.BlockSpec((tm, D_IN), lambda i: (i, 0)), keep_spec, keep_spec]
    out_shape = [jax.ShapeDtypeStruct((m, D_IN), BF16),
                 jax.ShapeDtypeStruct((nkeep * tm, D_ATTN), F32),
                 jax.ShapeDtypeStruct((nkeep * tm, D_ATTN), F32)]
    if emit_vn:
        out_specs.append(pl.BlockSpec((tm, D_GMLP), lambda i: (i, 0)))
        out_shape.append(jax.ShapeDtypeStruct((m, D_GMLP), F32))
    return pl.pallas_call(
        functools.partial(_inproj_kernel, keep_every=keep_every, emit_vn=emit_vn),
        grid=(nt,),
        in_specs=[pl.BlockSpec((tm, D_MODEL), lambda i: (i, 0)),
                  vec(D_MODEL),
                  pl.BlockSpec((D_MODEL, D_IN), lambda i: (0, 0),
                               pipeline_mode=pl.Buffered(1)),
                  vec(HEAD_DIM), vec(HEAD_DIM), vec(GROUP_DIM), vec(GROUP_DIM)],
        out_specs=out_specs,
        out_shape=out_shape,
        compiler_params=pltpu.CompilerParams(
            dimension_semantics=("arbitrary",), vmem_limit_bytes=VMEM_LIMIT_BYTES),
        name="inproj",
    )(x2d, nmg, w_in, qg, kg, lng, lnb)


def _softmax_pv(s, v):
    m = jnp.max(s, axis=-1, keepdims=True)
    p = jnp.exp(s - m)
    l = jnp.sum(p, axis=-1, keepdims=True)
    return _dot(p.astype(BF16), v) * (1.0 / l)


def _qk(q, k):
    return lax.dot_general(q, k, (((1,), (1,)), ((), ())), preferred_element_type=F32)


def _attn_prompt_kernel(q_ref, k_ref, v_ref, bm_ref, o_ref, kp_ref, vp_ref):
    seq = q_ref.shape[0]
    zeros = jnp.zeros((ATTN_WINDOW, HEAD_DIM), BF16)
    kp_ref[0:ATTN_WINDOW, :] = zeros
    vp_ref[0:ATTN_WINDOW, :] = zeros
    kp_ref[ATTN_WINDOW:, :] = k_ref[...]
    vp_ref[ATTN_WINDOW:, :] = v_ref[...]
    scale = HEAD_DIM ** -0.5

    def body(gi, carry):
        q0 = pl.multiple_of(gi * GQ, GQ)
        q = q_ref[pl.ds(q0, GQ), :]
        kb = kp_ref[pl.ds(q0, GK), :]
        vb = vp_ref[pl.ds(q0, GK), :]
        s = _qk(q, kb) * scale + bm_ref[0]
        col = lax.broadcasted_iota(jnp.int32, (GQ, GK), 1)
        s = jnp.where(col >= ATTN_WINDOW - q0, s, NEG_INF)
        o_ref[pl.ds(q0, GQ), :] = _softmax_pv(s, vb).astype(BF16)
        return carry

    lax.fori_loop(0, seq // GQ, body, 0)


def _attn_prompt(z, bm, *, batch, seq):
    return pl.pallas_call(
        _attn_prompt_kernel,
        grid=(batch, N_HEADS),
        in_specs=[pl.BlockSpec((seq, HEAD_DIM), lambda b, h: (b, h)),
                  pl.BlockSpec((seq, HEAD_DIM), lambda b, h: (b, N_HEADS + h)),
                  pl.BlockSpec((seq, HEAD_DIM), lambda b, h: (b, 2 * N_HEADS + h)),
                  pl.BlockSpec((1, GQ, GK), lambda b, h: (h, 0, 0))],
        out_specs=pl.BlockSpec((seq, HEAD_DIM), lambda b, h: (b, h)),
        out_shape=jax.ShapeDtypeStruct((batch * seq, D_ATTN), BF16),
        scratch_shapes=[pltpu.VMEM((seq + ATTN_WINDOW, HEAD_DIM), BF16),
                        pltpu.VMEM((seq + ATTN_WINDOW, HEAD_DIM), BF16)],
        compiler_params=pltpu.CompilerParams(
            dimension_semantics=("arbitrary", "arbitrary"),
            vmem_limit_bytes=VMEM_LIMIT_BYTES),
        name="attn_prompt",
    )(z, z, z, bm)


def _attn_sample_kernel(q_ref, k_ref, v_ref, bias_ref, o_ref):
    scale = HEAD_DIM ** -0.5
    for h in range(N_HEADS):
        lo = h * HEAD_DIM
        q = q_ref[:, lo:lo + HEAD_DIM]
        s = _qk(q, k_ref[0, :, lo:lo + HEAD_DIM]) * scale + bias_ref[h]
        o = _softmax_pv(s, v_ref[0, :, lo:lo + HEAD_DIM])
        o_ref[:, lo:lo + HEAD_DIM] = o.astype(BF16)


def _attn_sample(z, k_all, v_all, bias, *, batch, t):
    nk = k_all.shape[1]
    return pl.pallas_call(
        _attn_sample_kernel,
        grid=(batch,),
        in_specs=[pl.BlockSpec((t, D_ATTN), lambda b: (b, 0)),
                  pl.BlockSpec((1, nk, D_ATTN), lambda b: (b, 0, 0)),
                  pl.BlockSpec((1, nk, D_ATTN), lambda b: (b, 0, 0)),
                  pl.BlockSpec((N_HEADS, t, nk), lambda b: (0, 0, 0))],
        out_specs=pl.BlockSpec((t, D_ATTN), lambda b: (b, 0)),
        out_shape=jax.ShapeDtypeStruct((batch * t, D_ATTN), BF16),
        compiler_params=pltpu.CompilerParams(
            dimension_semantics=("arbitrary",), vmem_limit_bytes=VMEM_LIMIT_BYTES),
        name="attn_sample",
    )(z, k_all, v_all, bias)


def _outproj_kernel(oa_ref, u_ref, vn_ref, x_ref, wout_ref, ws_ref, bsb_ref, g_ref,
                    h_ref, n2_ref, ob_ref, *, chunk):
    tm = x_ref.shape[0]
    row = lax.broadcasted_iota(jnp.int32, (chunk, chunk), 0)
    col = lax.broadcasted_iota(jnp.int32, (chunk, chunk), 1)
    tri = row >= col
    for g in range(N_GROUPS):
        lo = g * GROUP_DIM
        wsg = jnp.where(tri, ws_ref[g], 0.0).astype(BF16)
        for c in range(tm // chunk):
            r0 = c * chunk
            vs = _dot(wsg, vn_ref[r0:r0 + chunk, lo:lo + GROUP_DIM]) + bsb_ref[g]
            ob = u_ref[r0:r0 + chunk, lo:lo + GROUP_DIM].astype(F32) * vs
            ob_ref[r0:r0 + chunk, lo:lo + GROUP_DIM] = ob.astype(BF16)
    h = (x_ref[...] + _dot(oa_ref[...], wout_ref[0:D_ATTN, :])
         + _dot(ob_ref[...], wout_ref[D_ATTN:, :]))
    h_ref[...] = h
    ms = jnp.mean(h * h, axis=-1, keepdims=True)
    n2_ref[...] = (h * lax.rsqrt(ms + EPS) * g_ref[...]).astype(BF16)


def _outproj(oa, z, x2d, w_out, ws, bsb, nfg, *, chunk):
    m = x2d.shape[0]
    tm = 512
    return pl.pallas_call(
        functools.partial(_outproj_kernel, chunk=chunk),
        grid=(m // tm,),
        in_specs=[pl.BlockSpec((tm, D_ATTN), lambda i: (i, 0)),
                  pl.BlockSpec((tm, D_GMLP), lambda i: (i, 3)),
                  pl.BlockSpec((tm, D_GMLP), lambda i: (i, 4)),
                  pl.BlockSpec((tm, D_MODEL), lambda i: (i, 0)),
                  pl.BlockSpec((D_MODEL, D_MODEL), lambda i: (0, 0),
                               pipeline_mode=pl.Buffered(1)),
                  pl.BlockSpec((N_GROUPS, chunk, chunk), lambda i: (0, 0, 0)),
                  pl.BlockSpec((N_GROUPS, chunk, GROUP_DIM), lambda i: (0, 0, 0)),
                  pl.BlockSpec((1, D_MODEL), lambda i: (0, 0))],
        out_specs=[pl.BlockSpec((tm, D_MODEL), lambda i: (i, 0)),
                   pl.BlockSpec((tm, D_MODEL), lambda i: (i, 0))],
        out_shape=[jax.ShapeDtypeStruct((m, D_MODEL), F32),
                   jax.ShapeDtypeStruct((m, D_MODEL), BF16)],
        scratch_shapes=[pltpu.VMEM((tm, D_GMLP), BF16)],
        compiler_params=pltpu.CompilerParams(
            dimension_semantics=("arbitrary",), vmem_limit_bytes=VMEM_LIMIT_BYTES),
        name="outproj",
    )(oa, z, z, x2d, w_out, ws, bsb, nfg)


def _up_kernel(n2_ref, wa_ref, wg_ref, cw_ref, cb_ref, st_ref, m_ref, cs_ref,
               a_ref, carry_ref, *, nseg, tiles_per_seq):
    i = pl.program_id(1)
    tm = n2_ref.shape[0]
    sl = tm // nseg
    hist = SUBLANES
    a = _dot(n2_ref[...], wa_ref[...])
    gate = _dot(n2_ref[...], wg_ref[...])
    w0 = cw_ref[0:1, :]
    w1 = cw_ref[1:2, :]
    w2 = cw_ref[2:3, :]
    for s in range(nseg):
        if nseg == 1:
            first = (i % tiles_per_seq) == 0
            a_ref[s, 0:hist, :] = jnp.where(first, 0.0, carry_ref[...])
        else:
            a_ref[s, 0:hist, :] = st_ref[s]
        a_ref[s, hist:hist + sl, :] = a[s * sl:(s + 1) * sl, :]
        ac = (cb_ref[...] + w0 * a_ref[s, hist - 2:hist - 2 + sl, :]
              + w1 * a_ref[s, hist - 1:hist - 1 + sl, :]
              + w2 * a_ref[s, hist:hist + sl, :])
        act = ac * (1.0 / (1.0 + jnp.exp(-ac)))
        m_ref[s * sl:(s + 1) * sl, :] = (act * gate[s * sl:(s + 1) * sl, :]).astype(BF16)
        tail = a_ref[s, sl:sl + hist, :]
        if nseg == 1:
            carry_ref[...] = tail
            cs_ref[0] = tail
        else:
            cs_ref[s] = tail


def _up(n2, wa, wg, cw, cb, state8, *, nseg, tiles_per_seq):
    m = n2.shape[0]
    tm = 1024
    nt = m // tm
    tn = FF_TILE
    nj = D_FF_PAD // tn
    sl = tm // nseg
    nstate = state8.shape[0]
    if nseg == 1:
        cs_rows = nt // tiles_per_seq
        cs_spec = pl.BlockSpec((1, SUBLANES, tn), lambda j, i: (i // tiles_per_seq, 0, j))
        st_spec = pl.BlockSpec((1, SUBLANES, tn), lambda j, i: (0, 0, j))
    else:
        cs_rows = nt * nseg
        cs_spec = pl.BlockSpec((nseg, SUBLANES, tn), lambda j, i: (i, 0, j))
        st_spec = pl.BlockSpec((nseg, SUBLANES, tn), lambda j, i: (i, 0, j))
    del nstate
    wspec = pl.BlockSpec((D_MODEL, tn), lambda j, i: (0, j), pipeline_mode=pl.Buffered(1))
    return pl.pallas_call(
        functools.partial(_up_kernel, nseg=nseg, tiles_per_seq=tiles_per_seq),
        grid=(nj, nt),
        in_specs=[pl.BlockSpec((tm, D_MODEL), lambda j, i: (i, 0)),
                  wspec, wspec,
                  pl.BlockSpec((CONV_W, tn), lambda j, i: (0, j)),
                  pl.BlockSpec((1, tn), lambda j, i: (0, j)),
                  st_spec],
        out_specs=[pl.BlockSpec((tm, tn), lambda j, i: (i, j)), cs_spec],
        out_shape=[jax.ShapeDtypeStruct((m, D_FF_PAD), BF16),
                   jax.ShapeDtypeStruct((cs_rows, SUBLANES, D_FF_PAD), F32)],
        scratch_shapes=[pltpu.VMEM((nseg, sl + SUBLANES, tn), F32),
                        pltpu.VMEM((SUBLANES, tn), F32)],
        compiler_params=pltpu.CompilerParams(
            dimension_semantics=("arbitrary", "arbitrary"),
            vmem_limit_bytes=VMEM_LIMIT_BYTES),
        name="up",
    )(n2, wa, wg, cw, cb, state8)


def _down_kernel(m_ref, w_ref, h_ref, y_ref):
    y_ref[...] = h_ref[...] + _dot(m_ref[...], w_ref[...])


def _down(mm, w_down, h):
    m = mm.shape[0]
    tm = 512
    return pl.pallas_call(
        _down_kernel,
        grid=(m // tm,),
        in_specs=[pl.BlockSpec((tm, D_FF_PAD), lambda i: (i, 0)),
                  pl.BlockSpec((D_FF_PAD, D_MODEL), lambda i: (0, 0),
                               pipeline_mode=pl.Buffered(1)),
                  pl.BlockSpec((tm, D_MODEL), lambda i: (i, 0))],
        out_specs=pl.BlockSpec((tm, D_MODEL), lambda i: (i, 0)),
        out_shape=jax.ShapeDtypeStruct((m, D_MODEL), F32),
        compiler_params=pltpu.CompilerParams(
            dimension_semantics=("arbitrary",), vmem_limit_bytes=VMEM_LIMIT_BYTES),
        name="down",
    )(mm, w_down, h)


def _band_bias(table):
    rel = jnp.clip((jnp.arange(CHUNK) + ATTN_WINDOW)[:, None] - jnp.arange(BAND)[None, :],
                   -REL_CLIP, REL_CLIP) + REL_CLIP
    return table[:, rel].astype(F32)


def _group_bias(bias):
    bm = jnp.full((N_HEADS, GQ, GK), NEG_INF, F32)
    for c in range(ATTN_GROUP):
        bm = bm.at[:, c * CHUNK:(c + 1) * CHUNK, c * CHUNK:c * CHUNK + BAND].set(bias)
    return bm


def _prep_weights(w_in, w_out, w_up, cw, cb, w_down):
    pad = D_FF_PAD - D_FF
    wa = jnp.pad(w_up[:, :D_FF], ((0, 0), (0, pad))).astype(BF16)
    wg = jnp.pad(w_up[:, D_FF:], ((0, 0), (0, pad))).astype(BF16)
    wd = jnp.pad(w_down, ((0, pad), (0, 0))).astype(BF16)
    cwp = jnp.pad(cw, ((0, 0), (0, pad)))
    cbp = jnp.pad(cb, ((0, pad),))[None, :]
    return w_in.astype(BF16), w_out.astype(BF16), wa, wg, cwp, cbp, wd


def _layer(x2d, seq, sample_cache, nmg, w_in, qg, kg, bias, lng, lnb, ws, bs, w_out, nfg,
           wa, wg, cw, cb, wd):
    m = x2d.shape[0]
    batch = m // seq
    is_sample = sample_cache is not None
    row = lambda v: v[None, :]
    if is_sample:
        z, kf, vf, vnf = _inproj(x2d, row(nmg), w_in, row(qg), row(kg), row(lng), row(lnb),
                                 keep_every=1, emit_vn=True)
        ck, cv, cst = sample_cache
        w_cache = ck.shape[1]
        k_all = jnp.concatenate([ck.reshape(batch, w_cache, D_ATTN).astype(BF16),
                                 z[:, D_ATTN:2 * D_ATTN].reshape(batch, seq, D_ATTN)], axis=1)
        v_all = jnp.concatenate([cv.reshape(batch, w_cache, D_ATTN).astype(BF16),
                                 z[:, 2 * D_ATTN:3 * D_ATTN].reshape(batch, seq, D_ATTN)], axis=1)
        oa = _attn_sample(z, k_all, v_all, bias[:, :, BAND - w_cache - seq:],
                          batch=batch, t=seq)
        chunk = seq
        state8 = jnp.pad(cst, ((0, 0), (SUBLANES - (CONV_W - 1), 0), (0, D_FF_PAD - D_FF)))
        nseg, tiles_per_seq = batch, 1
    else:
        keep_every = seq // 512
        z, kf, vf = _inproj(x2d, row(nmg), w_in, row(qg), row(kg), row(lng), row(lnb),
                            keep_every=keep_every, emit_vn=False)
        vnf = None
        oa = _attn_prompt(z, _group_bias(bias), batch=batch, seq=seq)
        chunk = GMLP_CHUNK
        state8 = jnp.zeros((1, SUBLANES, D_FF_PAD), F32)
        nseg, tiles_per_seq = 1, seq // 1024
    wsl = ws[:, :chunk, :chunk]
    bsb = jnp.broadcast_to(bs[:, :chunk, None], (N_GROUPS, chunk, GROUP_DIM))
    h, n2 = _outproj(oa, z, x2d, w_out, wsl, bsb, row(nfg), chunk=chunk)
    mm, cs = _up(n2, wa, wg, cw, cb, state8, nseg=nseg, tiles_per_seq=tiles_per_seq)
    y = _down(mm, wd, h)
    conv_state = cs[:, SUBLANES - (CONV_W - 1):, :D_FF]
    return y, kf, vf, vnf, conv_state


def kernel(x_prompt, x_sample, cache_attn_k, cache_attn_v, state_ffn_conv, norm_mix_g, w_in,
           q_norm_g, k_norm_g, rel_bias_table, gmlp_ln_g, gmlp_ln_b, gmlp_w_s, gmlp_b_s, w_out,
           norm_ffn_g, w_up, ffn_conv_w, ffn_conv_b, w_down):
    batch, seq, _ = x_prompt.shape
    dbatch, dseq, _ = x_sample.shape
    depth = w_in.shape[0]
    xp = x_prompt.reshape(batch * seq, D_MODEL)
    xs = x_sample.reshape(dbatch * dseq, D_MODEL)
    keep = min(ATTN_WINDOW, seq)
    outs = [[] for _ in range(7)]
    for l in range(depth):
        w_in_b, w_out_b, wa, wg, cw, cb, wd = _prep_weights(
            w_in[l], w_out[l], w_up[l], ffn_conv_w[l], ffn_conv_b[l], w_down[l])
        bias = _band_bias(rel_bias_table[l])
        shared = (norm_mix_g[l], w_in_b, q_norm_g[l], k_norm_g[l], bias, gmlp_ln_g[l],
                  gmlp_ln_b[l], gmlp_w_s[l], gmlp_b_s[l], w_out_b, norm_ffn_g[l],
                  wa, wg, cw, cb, wd)
        xp, kp, vp, _, cp = _layer(xp, seq, None, *shared)
        xs, ks, vs, gs, cs = _layer(
            xs, dseq, (cache_attn_k[l], cache_attn_v[l], state_ffn_conv[l]), *shared)
        outs[0].append(kp.reshape(batch, keep, N_HEADS, HEAD_DIM))
        outs[1].append(vp.reshape(batch, keep, N_HEADS, HEAD_DIM))
        outs[2].append(cp)
        outs[3].append(ks.reshape(dbatch, dseq, N_HEADS, HEAD_DIM))
        outs[4].append(vs.reshape(dbatch, dseq, N_HEADS, HEAD_DIM))
        outs[5].append(gs.reshape(dbatch, dseq, D_GMLP))
        outs[6].append(cs)
    return (xp.reshape(batch, seq, D_MODEL), xs.reshape(dbatch, dseq, D_MODEL),
            *[jnp.stack(o) for o in outs])
```

```python
import functools

import jax
import jax.numpy as jnp
from jax import lax
from jax.experimental import pallas as pl
from jax.experimental.pallas import tpu as pltpu

D_MODEL = 2048
CHUNK = 64
ATTN_WINDOW = 8 * CHUNK
BAND = ATTN_WINDOW + CHUNK
D_ATTN = D_MODEL // 2
HEAD_DIM = 128
N_HEADS = D_ATTN // HEAD_DIM
D_GMLP = D_MODEL - D_ATTN
N_GROUPS = 8
GROUP_DIM = D_GMLP // N_GROUPS
GMLP_CHUNK = 128
REL_CLIP = 128
D_FF = 5504
CONV_W = 3
D_IN = 3 * D_ATTN + 2 * D_GMLP
EPS = 1e-6
NEG_INF = -1e30
LOG2E = 1.4426950408889634

LANES = 128
SUBLANES = 8
VMEM_LIMIT_BYTES = 60 * 1024 * 1024

FF_TILE = 2816
D_FF_PAD = 2 * FF_TILE
ATTN_GROUP = 4
GQ = ATTN_GROUP * CHUNK
GK = GQ + ATTN_WINDOW

F32 = jnp.float32
BF16 = jnp.bfloat16


def _dot(a, b):
    return jnp.dot(a, b, preferred_element_type=F32)


def _gelu(x):
    return 0.5 * x * (1.0 + lax.erf(x * (0.5 ** 0.5)))


def _inproj_kernel(x_ref, nmg_ref, w_ref, qg_ref, kg_ref, lng_ref, lnb_ref,
                   z_ref, kf_ref, vf_ref, *rest, emit_vn):
    x = x_ref[...]
    ms = jnp.mean(x * x, axis=-1, keepdims=True)
    n = (x * lax.rsqrt(ms + EPS) * nmg_ref[...]).astype(BF16)

    def head_rms(zh, g):
        r = lax.rsqrt(jnp.mean(zh * zh, axis=-1, keepdims=True) + EPS)
        return zh * r * g

    sub = 2 * HEAD_DIM
    for sb in range(D_IN // sub):
        seg = (sb * sub) // D_ATTN
        zs = _dot(n, w_ref[:, sb * sub:(sb + 1) * sub])
        for hh in range(sub // HEAD_DIM):
            c0 = sb * sub + hh * HEAD_DIM
            lo = c0 - seg * D_ATTN
            zh = zs[:, hh * HEAD_DIM:(hh + 1) * HEAD_DIM]
            if seg == 0:
                out = head_rms(zh, qg_ref[...] * (HEAD_DIM ** -0.5 * LOG2E))
            elif seg == 1:
                out = head_rms(zh, kg_ref[...])
            elif seg == 2:
                out = zh
            elif seg == 3:
                out = _gelu(zh)
            else:
                ge = _gelu(zh)
                mu = jnp.mean(ge, axis=-1, keepdims=True)
                d = ge - mu
                var = jnp.mean(d * d, axis=-1, keepdims=True)
                out = d * lax.rsqrt(var + EPS) * lng_ref[...] + lnb_ref[...]
            z_ref[:, c0:c0 + HEAD_DIM] = out.astype(BF16)
            if seg == 1:
                kf_ref[:, lo:lo + HEAD_DIM] = out
            elif seg == 2:
                vf_ref[:, lo:lo + HEAD_DIM] = out
            elif seg == 4 and emit_vn:
                rest[0][:, lo:lo + HEAD_DIM] = out


def _inproj(x2d, nmg, w_in, qg, kg, lng, lnb, *, keep_every, emit_vn):
    m = x2d.shape[0]
    tm = 512
    nt = m // tm
    nkeep = nt // keep_every
    vec = lambda width: pl.BlockSpec((1, width), lambda i: (0, 0))
    keep_spec = pl.BlockSpec((tm, D_ATTN), lambda i: (i // keep_every, 0))
    out_specs = [pl.BlockSpec((tm, D_IN), lambda i: (i, 0)), keep_spec, keep_spec]
    out_shape = [jax.ShapeDtypeStruct((m, D_IN), BF16),
                 jax.ShapeDtypeStruct((nkeep * tm, D_ATTN), F32),
                 jax.ShapeDtypeStruct((nkeep * tm, D_ATTN), F32)]
    if emit_vn:
        out_specs.append(pl.BlockSpec((tm, D_GMLP), lambda i: (i, 0)))
        out_shape.append(jax.ShapeDtypeStruct((m, D_GMLP), F32))
    return pl.pallas_call(
        functools.partial(_inproj_kernel, emit_vn=emit_vn),
        grid=(nt,),
        in_specs=[pl.BlockSpec((tm, D_MODEL), lambda i: (i, 0)),
                  vec(D_MODEL),
                  pl.BlockSpec((D_MODEL, D_IN), lambda i: (0, 0),
                               pipeline_mode=pl.Buffered(1)),
                  vec(HEAD_DIM), vec(HEAD_DIM), vec(GROUP_DIM), vec(GROUP_DIM)],
        out_specs=out_specs,
        out_shape=out_shape,
        compiler_params=pltpu.CompilerParams(
            dimension_semantics=("arbitrary",), vmem_limit_bytes=VMEM_LIMIT_BYTES),
        name="inproj",
    )(x2d, nmg, w_in, qg, kg, lng, lnb)


def _softmax_pv(s, v):
    ps, ls = [], []
    for r0 in range(0, s.shape[0], CHUNK):
        sc = s[r0:r0 + CHUNK, :]
        m = jnp.max(sc, axis=-1, keepdims=True)
        p = jnp.exp2(sc - m)
        ls.append(jnp.sum(p, axis=-1, keepdims=True))
        ps.append(p.astype(BF16))
    p = ps[0] if len(ps) == 1 else jnp.concatenate(ps, axis=0)
    l = ls[0] if len(ls) == 1 else jnp.concatenate(ls, axis=0)
    return _dot(p, v) * (1.0 / l)


def _qk(q, k):
    return lax.dot_general(q, k, (((1,), (1,)), ((), ())), preferred_element_type=F32)


def _attn_prompt_kernel(q_ref, k_ref, v_ref, bm_ref, o_ref, kp_ref, vp_ref):
    seq = q_ref.shape[0]
    zeros = jnp.zeros((ATTN_WINDOW, HEAD_DIM), BF16)
    kp_ref[0:ATTN_WINDOW, :] = zeros
    vp_ref[0:ATTN_WINDOW, :] = zeros
    kp_ref[ATTN_WINDOW:, :] = k_ref[...]
    vp_ref[ATTN_WINDOW:, :] = v_ref[...]

    def group(q0, pre_stream):
        q = q_ref[pl.ds(q0, GQ), :]
        kb = kp_ref[pl.ds(q0, GK), :]
        vb = vp_ref[pl.ds(q0, GK), :]
        s = _qk(q, kb) + bm_ref[0]
        if pre_stream:
            col = lax.broadcasted_iota(jnp.int32, (GQ, GK), 1)
            s = jnp.where(col >= ATTN_WINDOW - q0, s, NEG_INF)
        o_ref[pl.ds(q0, GQ), :] = _softmax_pv(s, vb).astype(BF16)

    n_pre = ATTN_WINDOW // GQ
    for gi in range(n_pre):
        group(gi * GQ, True)

    def body(gi, carry):
        group(pl.multiple_of(gi * GQ, GQ), False)
        return carry

    lax.fori_loop(n_pre, seq // GQ, body, 0, unroll=True)


def _attn_prompt(z, bm, *, batch, seq):
    return pl.pallas_call(
        _attn_prompt_kernel,
        grid=(batch, N_HEADS),
        in_specs=[pl.BlockSpec((seq, HEAD_DIM), lambda b, h: (b, h)),
                  pl.BlockSpec((seq, HEAD_DIM), lambda b, h: (b, N_HEADS + h)),
                  pl.BlockSpec((seq, HEAD_DIM), lambda b, h: (b, 2 * N_HEADS + h)),
                  pl.BlockSpec((1, GQ, GK), lambda b, h: (h, 0, 0))],
        out_specs=pl.BlockSpec((seq, HEAD_DIM), lambda b, h: (b, h)),
        out_shape=jax.ShapeDtypeStruct((batch * seq, D_ATTN), BF16),
        scratch_shapes=[pltpu.VMEM((seq + ATTN_WINDOW, HEAD_DIM), BF16),
                        pltpu.VMEM((seq + ATTN_WINDOW, HEAD_DIM), BF16)],
        compiler_params=pltpu.CompilerParams(
            dimension_semantics=("arbitrary", "arbitrary"),
            vmem_limit_bytes=VMEM_LIMIT_BYTES),
        name="attn_prompt",
    )(z, z, z, bm)


def _attn_sample_kernel(q_ref, k_ref, v_ref, bias_ref, o_ref):
    for h in range(N_HEADS):
        lo = h * HEAD_DIM
        q = q_ref[:, lo:lo + HEAD_DIM]
        s = _qk(q, k_ref[0, :, lo:lo + HEAD_DIM]) + bias_ref[h]
        o = _softmax_pv(s, v_ref[0, :, lo:lo + HEAD_DIM])
        o_ref[:, lo:lo + HEAD_DIM] = o.astype(BF16)


def _attn_sample(z, k_all, v_all, bias, *, batch, t):
    nk = k_all.shape[1]
    return pl.pallas_call(
        _attn_sample_kernel,
        grid=(batch,),
        in_specs=[pl.BlockSpec((t, D_ATTN), lambda b: (b, 0)),
                  pl.BlockSpec((1, nk, D_ATTN), lambda b: (b, 0, 0)),
                  pl.BlockSpec((1, nk, D_ATTN), lambda b: (b, 0, 0)),
                  pl.BlockSpec((N_HEADS, t, nk), lambda b: (0, 0, 0))],
        out_specs=pl.BlockSpec((t, D_ATTN), lambda b: (b, 0)),
        out_shape=jax.ShapeDtypeStruct((batch * t, D_ATTN), BF16),
        compiler_params=pltpu.CompilerParams(
            dimension_semantics=("arbitrary",), vmem_limit_bytes=VMEM_LIMIT_BYTES),
        name="attn_sample",
    )(z, k_all, v_all, bias)


def _outproj_kernel(oa_ref, u_ref, vn_ref, x_ref, wout_ref, ws_ref, bsb_ref, g_ref,
                    h_ref, n2_ref, ob_ref, *, chunk):
    tm = x_ref.shape[0]
    row = lax.broadcasted_iota(jnp.int32, (chunk, chunk), 0)
    col = lax.broadcasted_iota(jnp.int32, (chunk, chunk), 1)
    tri = row >= col
    for g in range(N_GROUPS):
        lo = g * GROUP_DIM
        wsg = jnp.where(tri, ws_ref[g], 0.0).astype(BF16)
        for c in range(tm // chunk):
            r0 = c * chunk
            vs = _dot(wsg, vn_ref[r0:r0 + chunk, lo:lo + GROUP_DIM]) + bsb_ref[g]
            ob = u_ref[r0:r0 + chunk, lo:lo + GROUP_DIM].astype(F32) * vs
            ob_ref[r0:r0 + chunk, lo:lo + GROUP_DIM] = ob.astype(BF16)
    h = (x_ref[...] + _dot(oa_ref[...], wout_ref[0:D_ATTN, :])
         + _dot(ob_ref[...], wout_ref[D_ATTN:, :]))
    h_ref[...] = h
    ms = jnp.mean(h * h, axis=-1, keepdims=True)
    n2_ref[...] = (h * lax.rsqrt(ms + EPS) * g_ref[...]).astype(BF16)


def _outproj(oa, z, x2d, w_out, ws, bsb, nfg, *, chunk):
    m = x2d.shape[0]
    tm = 512
    return pl.pallas_call(
        functools.partial(_outproj_kernel, chunk=chunk),
        grid=(m // tm,),
        in_specs=[pl.BlockSpec((tm, D_ATTN), lambda i: (i, 0)),
                  pl.BlockSpec((tm, D_GMLP), lambda i: (i, 3)),
                  pl.BlockSpec((tm, D_GMLP), lambda i: (i, 4)),
                  pl.BlockSpec((tm, D_MODEL), lambda i: (i, 0)),
                  pl.BlockSpec((D_MODEL, D_MODEL), lambda i: (0, 0),
                               pipeline_mode=pl.Buffered(1)),
                  pl.BlockSpec((N_GROUPS, chunk, chunk), lambda i: (0, 0, 0)),
                  pl.BlockSpec((N_GROUPS, chunk, GROUP_DIM), lambda i: (0, 0, 0)),
                  pl.BlockSpec((1, D_MODEL), lambda i: (0, 0))],
        out_specs=[pl.BlockSpec((tm, D_MODEL), lambda i: (i, 0)),
                   pl.BlockSpec((tm, D_MODEL), lambda i: (i, 0))],
        out_shape=[jax.ShapeDtypeStruct((m, D_MODEL), F32),
                   jax.ShapeDtypeStruct((m, D_MODEL), BF16)],
        scratch_shapes=[pltpu.VMEM((tm, D_GMLP), BF16)],
        compiler_params=pltpu.CompilerParams(
            dimension_semantics=("arbitrary",), vmem_limit_bytes=VMEM_LIMIT_BYTES),
        name="outproj",
    )(oa, z, z, x2d, w_out, ws, bsb, nfg)


def _silu(x):
    return x * (1.0 / (1.0 + jnp.exp(-x)))


def _up_kernel(n2_ref, wa_ref, wg_ref, cw_ref, cb_ref, st_ref, m_ref, cs_ref,
               carry_ref, *, nseg, tiles_per_seq):
    i = pl.program_id(1)
    tm = n2_ref.shape[0]
    tn = wa_ref.shape[1]
    sl = tm // nseg
    hist = SUBLANES
    sub = 2 * LANES
    n2 = n2_ref[...]
    for sb in range(tn // sub):
        cs = slice(sb * sub, (sb + 1) * sub)
        a = _dot(n2, wa_ref[:, cs])
        gate = _dot(n2, wg_ref[:, cs])
        w0 = cw_ref[0:1, cs]
        w1 = cw_ref[1:2, cs]
        w2 = cw_ref[2:3, cs]
        cb = cb_ref[:, cs]
        ac = cb + w0 * pltpu.roll(a, 2, 0) + w1 * pltpu.roll(a, 1, 0) + w2 * a
        act = _silu(ac) * gate
        m_ref[:, cs] = act.astype(BF16)
        for s in range(nseg):
            r0 = s * sl
            if nseg == 1:
                first = (i % tiles_per_seq) == 0
                prev = jnp.where(first, 0.0, carry_ref[:, cs])
            else:
                prev = st_ref[s, :, cs]
            a8 = a[r0:r0 + hist, :]
            ext = jnp.concatenate([prev, a8], axis=0)
            a_m1 = pltpu.roll(ext, 1, 0)[hist:, :]
            a_m2 = pltpu.roll(ext, 2, 0)[hist:, :]
            act8 = _silu(cb + w0 * a_m2 + w1 * a_m1 + w2 * a8) * gate[r0:r0 + hist, :]
            head = jnp.concatenate([act8, act[r0 + hist:r0 + 2 * hist, :]], axis=0)
            m_ref[r0:r0 + 2 * hist, cs] = head.astype(BF16)
            tail = a[r0 + sl - hist:r0 + sl, :]
            if nseg == 1:
                carry_ref[:, cs] = tail
                cs_ref[0, :, cs] = tail
            else:
                cs_ref[s, :, cs] = tail


def _up(n2, wa, wg, cw, cb, state8, *, nseg, tiles_per_seq):
    m = n2.shape[0]
    tm = 1024
    nt = m // tm
    tn = FF_TILE
    nj = D_FF_PAD // tn
    if nseg == 1:
        cs_rows = nt // tiles_per_seq
        cs_spec = pl.BlockSpec((1, SUBLANES, tn), lambda j, i: (i // tiles_per_seq, 0, j))
        st_spec = pl.BlockSpec((1, SUBLANES, tn), lambda j, i: (0, 0, j))
    else:
        cs_rows = nt * nseg
        cs_spec = pl.BlockSpec((nseg, SUBLANES, tn), lambda j, i: (i, 0, j))
        st_spec = pl.BlockSpec((nseg, SUBLANES, tn), lambda j, i: (i, 0, j))
    wspec = pl.BlockSpec((D_MODEL, tn), lambda j, i: (0, j), pipeline_mode=pl.Buffered(1))
    return pl.pallas_call(
        functools.partial(_up_kernel, nseg=nseg, tiles_per_seq=tiles_per_seq),
        grid=(nj, nt),
        in_specs=[pl.BlockSpec((tm, D_MODEL), lambda j, i: (i, 0)),
                  wspec, wspec,
                  pl.BlockSpec((CONV_W, tn), lambda j, i: (0, j)),
                  pl.BlockSpec((1, tn), lambda j, i: (0, j)),
                  st_spec],
        out_specs=[pl.BlockSpec((tm, tn), lambda j, i: (i, j)), cs_spec],
        out_shape=[jax.ShapeDtypeStruct((m, D_FF_PAD), BF16),
                   jax.ShapeDtypeStruct((cs_rows, SUBLANES, D_FF_PAD), F32)],
        scratch_shapes=[pltpu.VMEM((SUBLANES, tn), F32)],
        compiler_params=pltpu.CompilerParams(
            dimension_semantics=("arbitrary", "arbitrary"),
            vmem_limit_bytes=VMEM_LIMIT_BYTES),
        name="up",
    )(n2, wa, wg, cw, cb, state8)


def _down_kernel(m_ref, w_ref, h_ref, y_ref):
    y_ref[...] = h_ref[...] + _dot(m_ref[...], w_ref[...])


def _down(mm, w_down, h):
    m = mm.shape[0]
    tm = 512
    return pl.pallas_call(
        _down_kernel,
        grid=(m // tm,),
        in_specs=[pl.BlockSpec((tm, D_FF_PAD), lambda i: (i, 0)),
                  pl.BlockSpec((D_FF_PAD, D_MODEL), lambda i: (0, 0),
                               pipeline_mode=pl.Buffered(1)),
                  pl.BlockSpec((tm, D_MODEL), lambda i: (i, 0))],
        out_specs=pl.BlockSpec((tm, D_MODEL), lambda i: (i, 0)),
        out_shape=jax.ShapeDtypeStruct((m, D_MODEL), F32),
        compiler_params=pltpu.CompilerParams(
            dimension_semantics=("arbitrary",), vmem_limit_bytes=VMEM_LIMIT_BYTES),
        name="down",
    )(mm, w_down, h)


def _band_bias(table):
    d = jnp.arange(-(CHUNK - 1), BAND)
    e = table[:, jnp.clip(ATTN_WINDOW - d, -REL_CLIP, REL_CLIP) + REL_CLIP].astype(F32)
    n = e.shape[1]
    ep = jnp.pad(e, ((0, 0), (0, 1)))
    toep = jnp.tile(ep, (1, CHUNK))[:, :CHUNK * n].reshape(N_HEADS, CHUNK, n)
    return toep[:, :, CHUNK - 1:CHUNK - 1 + BAND] * LOG2E


def _group_bias(bias):
    rows = [jnp.pad(bias, ((0, 0), (0, 0), (c * CHUNK, GK - BAND - c * CHUNK)),
                    constant_values=NEG_INF) for c in range(ATTN_GROUP)]
    return jnp.concatenate(rows, axis=1)


def _prep_weights(w_in, w_out, w_up, cw, cb, w_down):
    pad = D_FF_PAD - D_FF
    wa = jnp.pad(w_up[:, :D_FF], ((0, 0), (0, pad))).astype(BF16)
    wg = jnp.pad(w_up[:, D_FF:], ((0, 0), (0, pad))).astype(BF16)
    wd = jnp.pad(w_down, ((0, pad), (0, 0))).astype(BF16)
    cwp = jnp.pad(cw, ((0, 0), (0, pad)))
    cbp = jnp.pad(cb, ((0, pad),))[None, :]
    return w_in.astype(BF16), w_out.astype(BF16), wa, wg, cwp, cbp, wd


def _layer(x2d, seq, sample_cache, nmg, w_in, qg, kg, bias, lng, lnb, ws, bs, w_out, nfg,
           wa, wg, cw, cb, wd):
    m = x2d.shape[0]
    batch = m // seq
    is_sample = sample_cache is not None
    row = lambda v: v[None, :]
    if is_sample:
        z, kf, vf, vnf = _inproj(x2d, row(nmg), w_in, row(qg), row(kg), row(lng), row(lnb),
                                 keep_every=1, emit_vn=True)
        ck, cv, cst = sample_cache
        w_cache = ck.shape[1]
        k_all = jnp.concatenate([ck.reshape(batch, w_cache, D_ATTN).astype(BF16),
                                 z[:, D_ATTN:2 * D_ATTN].reshape(batch, seq, D_ATTN)], axis=1)
        v_all = jnp.concatenate([cv.reshape(batch, w_cache, D_ATTN).astype(BF16),
                                 z[:, 2 * D_ATTN:3 * D_ATTN].reshape(batch, seq, D_ATTN)], axis=1)
        oa = _attn_sample(z, k_all, v_all, bias[:, :, BAND - w_cache - seq:],
                          batch=batch, t=seq)
        chunk = seq
        state8 = jnp.pad(cst, ((0, 0), (SUBLANES - (CONV_W - 1), 0), (0, D_FF_PAD - D_FF)))
        nseg, tiles_per_seq = batch, 1
    else:
        keep_every = seq // 512
        z, kf, vf = _inproj(x2d, row(nmg), w_in, row(qg), row(kg), row(lng), row(lnb),
                            keep_every=keep_every, emit_vn=False)
        vnf = None
        oa = _attn_prompt(z, _group_bias(bias), batch=batch, seq=seq)
        chunk = GMLP_CHUNK
        state8 = jnp.zeros((1, SUBLANES, D_FF_PAD), F32)
        nseg, tiles_per_seq = 1, seq // 1024
    wsl = ws[:, :chunk, :chunk]
    bsb = jnp.broadcast_to(bs[:, :chunk, None], (N_GROUPS, chunk, GROUP_DIM))
    h, n2 = _outproj(oa, z, x2d, w_out, wsl, bsb, row(nfg), chunk=chunk)
    mm, cs = _up(n2, wa, wg, cw, cb, state8, nseg=nseg, tiles_per_seq=tiles_per_seq)
    y = _down(mm, wd, h)
    conv_state = cs[:, SUBLANES - (CONV_W - 1):, :D_FF]
    return y, kf, vf, vnf, conv_state


def kernel(x_prompt, x_sample, cache_attn_k, cache_attn_v, state_ffn_conv, norm_mix_g, w_in,
           q_norm_g, k_norm_g, rel_bias_table, gmlp_ln_g, gmlp_ln_b, gmlp_w_s, gmlp_b_s, w_out,
           norm_ffn_g, w_up, ffn_conv_w, ffn_conv_b, w_down):
    batch, seq, _ = x_prompt.shape
    dbatch, dseq, _ = x_sample.shape
    depth = w_in.shape[0]
    xp = x_prompt.reshape(batch * seq, D_MODEL)
    xs = x_sample.reshape(dbatch * dseq, D_MODEL)
    keep = min(ATTN_WINDOW, seq)
    outs = [[] for _ in range(7)]
    for l in range(depth):
        w_in_b, w_out_b, wa, wg, cw, cb, wd = _prep_weights(
            w_in[l], w_out[l], w_up[l], ffn_conv_w[l], ffn_conv_b[l], w_down[l])
        bias = _band_bias(rel_bias_table[l])
        shared = (norm_mix_g[l], w_in_b, q_norm_g[l], k_norm_g[l], bias, gmlp_ln_g[l],
                  gmlp_ln_b[l], gmlp_w_s[l], gmlp_b_s[l], w_out_b, norm_ffn_g[l],
                  wa, wg, cw, cb, wd)
        xp, kp, vp, _, cp = _layer(xp, seq, None, *shared)
        xs, ks, vs, gs, cs = _layer(
            xs, dseq, (cache_attn_k[l], cache_attn_v[l], state_ffn_conv[l]), *shared)
        outs[0].append(kp.reshape(batch, keep, N_HEADS, HEAD_DIM))
        outs[1].append(vp.reshape(batch, keep, N_HEADS, HEAD_DIM))
        outs[2].append(cp)
        outs[3].append(ks.reshape(dbatch, dseq, N_HEADS, HEAD_DIM))
        outs[4].append(vs.reshape(dbatch, dseq, N_HEADS, HEAD_DIM))
        outs[5].append(gs.reshape(dbatch, dseq, D_GMLP))
        outs[6].append(cs)
    return (xp.reshape(batch, seq, D_MODEL), xs.reshape(dbatch, dseq, D_MODEL),
            *[jnp.stack(o) for o in outs])
```

```python
import functools

import jax
import jax.numpy as jnp
from jax import lax
from jax.experimental import pallas as pl
from jax.experimental.pallas import tpu as pltpu

D_MODEL = 2048
CHUNK = 64
ATTN_WINDOW = 8 * CHUNK
BAND = ATTN_WINDOW + CHUNK
D_ATTN = D_MODEL // 2
HEAD_DIM = 128
N_HEADS = D_ATTN // HEAD_DIM
D_GMLP = D_MODEL - D_ATTN
N_GROUPS = 8
GROUP_DIM = D_GMLP // N_GROUPS
GMLP_CHUNK = 128
REL_CLIP = 128
D_FF = 5504
CONV_W = 3
D_IN = 3 * D_ATTN + 2 * D_GMLP
EPS = 1e-6
NEG_INF = -1e30
LOG2E = 1.4426950408889634

LANES = 128
SUBLANES = 8
VMEM_LIMIT_BYTES = 60 * 1024 * 1024

FF_TILE = 2816
D_FF_PAD = 2 * FF_TILE
UP_SUB = 256
UP_ROWS = 64
UP_DOT_ROWS = 256
ATTN_GROUP = 4
GQ = ATTN_GROUP * CHUNK
GK = GQ + ATTN_WINDOW

F32 = jnp.float32
BF16 = jnp.bfloat16


def _dot(a, b):
    return jnp.dot(a, b, preferred_element_type=F32)


def _gelu(x):
    return 0.5 * x * (1.0 + lax.erf(x * (0.5 ** 0.5)))


def _inproj_kernel(x_ref, nmg_ref, w_ref, qg_ref, kg_ref, lng_ref, lnb_ref,
                   z_ref, kf_ref, vf_ref, *rest, emit_vn):
    x = x_ref[...]
    ms = jnp.mean(x * x, axis=-1, keepdims=True)
    n = (x * lax.rsqrt(ms + EPS) * nmg_ref[...]).astype(BF16)

    def head_rms(zh, g):
        r = lax.rsqrt(jnp.mean(zh * zh, axis=-1, keepdims=True) + EPS)
        return zh * r * g

    sub = 4 * HEAD_DIM
    order = sorted(range(D_IN // sub), key=lambda sb: (4, 3, 0, 1, 2).index(sb * sub // D_ATTN))
    for sb in order:
        seg = (sb * sub) // D_ATTN
        zs = _dot(n, w_ref[:, sb * sub:(sb + 1) * sub])
        for hh in range(sub // HEAD_DIM):
            c0 = sb * sub + hh * HEAD_DIM
            lo = c0 - seg * D_ATTN
            zh = zs[:, hh * HEAD_DIM:(hh + 1) * HEAD_DIM]
            if seg == 0:
                out = head_rms(zh, qg_ref[...] * (HEAD_DIM ** -0.5 * LOG2E))
            elif seg == 1:
                out = head_rms(zh, kg_ref[...])
            elif seg == 2:
                out = zh
            elif seg == 3:
                out = _gelu(zh)
            else:
                ge = _gelu(zh)
                mu = jnp.mean(ge, axis=-1, keepdims=True)
                d = ge - mu
                var = jnp.mean(d * d, axis=-1, keepdims=True)
                out = d * lax.rsqrt(var + EPS) * lng_ref[...] + lnb_ref[...]
            z_ref[:, c0:c0 + HEAD_DIM] = out.astype(BF16)
            if seg == 1:
                kf_ref[:, lo:lo + HEAD_DIM] = out
            elif seg == 2:
                vf_ref[:, lo:lo + HEAD_DIM] = out
            elif seg == 4 and emit_vn:
                rest[0][:, lo:lo + HEAD_DIM] = out


def _inproj(x2d, nmg, w_in, qg, kg, lng, lnb, *, keep_every, emit_vn):
    m = x2d.shape[0]
    tm = 512
    nt = m // tm
    nkeep = nt // keep_every
    vec = lambda width: pl.BlockSpec((1, width), lambda i: (0, 0))
    keep_spec = pl.BlockSpec((tm, D_ATTN), lambda i: (i // keep_every, 0))
    out_specs = [pl.BlockSpec((tm, D_IN), lambda i: (i, 0)), keep_spec, keep_spec]
    out_shape = [jax.ShapeDtypeStruct((m, D_IN), BF16),
                 jax.ShapeDtypeStruct((nkeep * tm, D_ATTN), F32),
                 jax.ShapeDtypeStruct((nkeep * tm, D_ATTN), F32)]
    if emit_vn:
        out_specs.append(pl.BlockSpec((tm, D_GMLP), lambda i: (i, 0)))
        out_shape.append(jax.ShapeDtypeStruct((m, D_GMLP), F32))
    return pl.pallas_call(
        functools.partial(_inproj_kernel, emit_vn=emit_vn),
        grid=(nt,),
        in_specs=[pl.BlockSpec((tm, D_MODEL), lambda i: (i, 0)),
                  vec(D_MODEL),
                  pl.BlockSpec((D_MODEL, D_IN), lambda i: (0, 0),
                               pipeline_mode=pl.Buffered(1)),
                  vec(HEAD_DIM), vec(HEAD_DIM), vec(GROUP_DIM), vec(GROUP_DIM)],
        out_specs=out_specs,
        out_shape=out_shape,
        compiler_params=pltpu.CompilerParams(
            dimension_semantics=("arbitrary",), vmem_limit_bytes=VMEM_LIMIT_BYTES),
        name="inproj",
    )(x2d, nmg, w_in, qg, kg, lng, lnb)


def _softmax_pv(s, v):
    ps = []
    for r0 in range(0, s.shape[0], CHUNK):
        sc = s[r0:r0 + CHUNK, :]
        m = jnp.max(sc, axis=-1, keepdims=True)
        ps.append(jnp.exp2(sc - m).astype(BF16))
    p = ps[0] if len(ps) == 1 else jnp.concatenate(ps, axis=0)
    ol = _dot(p, v)
    return ol[:, :HEAD_DIM] * (1.0 / ol[:, HEAD_DIM:])


def _qk(q, k):
    return lax.dot_general(q, k, (((1,), (1,)), ((), ())), preferred_element_type=F32)


def _attn_prompt_kernel(q_ref, k_ref, v_ref, bm_ref, o_ref, kp_ref, vp_ref):
    seq = q_ref.shape[0]
    zeros = jnp.zeros((ATTN_WINDOW, HEAD_DIM), BF16)
    kp_ref[0:ATTN_WINDOW, :] = zeros
    vp_ref[0:ATTN_WINDOW, 0:HEAD_DIM] = zeros
    kp_ref[ATTN_WINDOW:, :] = k_ref[...]
    vp_ref[ATTN_WINDOW:, 0:HEAD_DIM] = v_ref[...]
    vp_ref[:, HEAD_DIM:] = jnp.ones((seq + ATTN_WINDOW, HEAD_DIM), BF16)

    def group(q0, pre_stream):
        q = q_ref[pl.ds(q0, GQ), :]
        kb = kp_ref[pl.ds(q0, GK), :]
        vb = vp_ref[pl.ds(q0, GK), :]
        s = _qk(q, kb) + bm_ref[0]
        if pre_stream:
            col = lax.broadcasted_iota(jnp.int32, (GQ, GK), 1)
            s = jnp.where(col >= ATTN_WINDOW - q0, s, NEG_INF)
        o_ref[pl.ds(q0, GQ), :] = _softmax_pv(s, vb).astype(BF16)

    n_pre = ATTN_WINDOW // GQ
    for gi in range(n_pre):
        group(gi * GQ, True)

    def body(gi, carry):
        group(pl.multiple_of(gi * GQ, GQ), False)
        return carry

    lax.fori_loop(n_pre, seq // GQ, body, 0, unroll=True)


def _attn_prompt(z, bm, *, batch, seq):
    return pl.pallas_call(
        _attn_prompt_kernel,
        grid=(batch, N_HEADS),
        in_specs=[pl.BlockSpec((seq, HEAD_DIM), lambda b, h: (b, h)),
                  pl.BlockSpec((seq, HEAD_DIM), lambda b, h: (b, N_HEADS + h)),
                  pl.BlockSpec((seq, HEAD_DIM), lambda b, h: (b, 2 * N_HEADS + h)),
                  pl.BlockSpec((1, GQ, GK), lambda b, h: (h, 0, 0))],
        out_specs=pl.BlockSpec((seq, HEAD_DIM), lambda b, h: (b, h)),
        out_shape=jax.ShapeDtypeStruct((batch * seq, D_ATTN), BF16),
        scratch_shapes=[pltpu.VMEM((seq + ATTN_WINDOW, HEAD_DIM), BF16),
                        pltpu.VMEM((seq + ATTN_WINDOW, 2 * HEAD_DIM), BF16)],
        compiler_params=pltpu.CompilerParams(
            dimension_semantics=("arbitrary", "arbitrary"),
            vmem_limit_bytes=VMEM_LIMIT_BYTES),
        name="attn_prompt",
    )(z, z, z, bm)


def _attn_sample_kernel(q_ref, k_ref, v_ref, bias_ref, o_ref):
    ones = jnp.ones((k_ref.shape[1], HEAD_DIM), BF16)
    for h in range(N_HEADS):
        lo = h * HEAD_DIM
        q = q_ref[:, lo:lo + HEAD_DIM]
        s = _qk(q, k_ref[0, :, lo:lo + HEAD_DIM]) + bias_ref[h]
        o = _softmax_pv(s, jnp.concatenate([v_ref[0, :, lo:lo + HEAD_DIM], ones], axis=1))
        o_ref[:, lo:lo + HEAD_DIM] = o.astype(BF16)


def _attn_sample(z, k_all, v_all, bias, *, batch, t):
    nk = k_all.shape[1]
    return pl.pallas_call(
        _attn_sample_kernel,
        grid=(batch,),
        in_specs=[pl.BlockSpec((t, D_ATTN), lambda b: (b, 0)),
                  pl.BlockSpec((1, nk, D_ATTN), lambda b: (b, 0, 0)),
                  pl.BlockSpec((1, nk, D_ATTN), lambda b: (b, 0, 0)),
                  pl.BlockSpec((N_HEADS, t, nk), lambda b: (0, 0, 0))],
        out_specs=pl.BlockSpec((t, D_ATTN), lambda b: (b, 0)),
        out_shape=jax.ShapeDtypeStruct((batch * t, D_ATTN), BF16),
        compiler_params=pltpu.CompilerParams(
            dimension_semantics=("arbitrary",), vmem_limit_bytes=VMEM_LIMIT_BYTES),
        name="attn_sample",
    )(z, k_all, v_all, bias)


def _outproj_kernel(oa_ref, u_ref, vn_ref, x_ref, wout_ref, ws_ref, bsb_ref, g_ref,
                    h_ref, n2_ref, ob_ref, *, chunk):
    tm = x_ref.shape[0]
    row = lax.broadcasted_iota(jnp.int32, (chunk, chunk), 0)
    col = lax.broadcasted_iota(jnp.int32, (chunk, chunk), 1)
    tri = row >= col
    for g in range(N_GROUPS):
        lo = g * GROUP_DIM
        wsg = jnp.where(tri, ws_ref[g], 0.0).astype(BF16)
        for c in range(tm // chunk):
            r0 = c * chunk
            vs = _dot(wsg, vn_ref[r0:r0 + chunk, lo:lo + GROUP_DIM]) + bsb_ref[g]
            ob = u_ref[r0:r0 + chunk, lo:lo + GROUP_DIM].astype(F32) * vs
            ob_ref[r0:r0 + chunk, lo:lo + GROUP_DIM] = ob.astype(BF16)
    h = (x_ref[...] + _dot(oa_ref[...], wout_ref[0:D_ATTN, :])
         + _dot(ob_ref[...], wout_ref[D_ATTN:, :]))
    h_ref[...] = h
    ms = jnp.mean(h * h, axis=-1, keepdims=True)
    n2_ref[...] = (h * lax.rsqrt(ms + EPS) * g_ref[...]).astype(BF16)


def _outproj(oa, z, x2d, w_out, ws, bsb, nfg, *, chunk):
    m = x2d.shape[0]
    tm = 512
    return pl.pallas_call(
        functools.partial(_outproj_kernel, chunk=chunk),
        grid=(m // tm,),
        in_specs=[pl.BlockSpec((tm, D_ATTN), lambda i: (i, 0)),
                  pl.BlockSpec((tm, D_GMLP), lambda i: (i, 3)),
                  pl.BlockSpec((tm, D_GMLP), lambda i: (i, 4)),
                  pl.BlockSpec((tm, D_MODEL), lambda i: (i, 0)),
                  pl.BlockSpec((D_MODEL, D_MODEL), lambda i: (0, 0),
                               pipeline_mode=pl.Buffered(1)),
                  pl.BlockSpec((N_GROUPS, chunk, chunk), lambda i: (0, 0, 0)),
                  pl.BlockSpec((N_GROUPS, chunk, GROUP_DIM), lambda i: (0, 0, 0)),
                  pl.BlockSpec((1, D_MODEL), lambda i: (0, 0))],
        out_specs=[pl.BlockSpec((tm, D_MODEL), lambda i: (i, 0)),
                   pl.BlockSpec((tm, D_MODEL), lambda i: (i, 0))],
        out_shape=[jax.ShapeDtypeStruct((m, D_MODEL), F32),
                   jax.ShapeDtypeStruct((m, D_MODEL), BF16)],
        scratch_shapes=[pltpu.VMEM((tm, D_GMLP), BF16)],
        compiler_params=pltpu.CompilerParams(
            dimension_semantics=("arbitrary",), vmem_limit_bytes=VMEM_LIMIT_BYTES),
        name="outproj",
    )(oa, z, z, x2d, w_out, ws, bsb, nfg)


def _silu(x):
    return x * (1.0 / (1.0 + jnp.exp(-x)))


def _up_kernel(n2_ref, w_ref, cw_ref, cb_ref, st_ref, m_ref, cs_ref,
               carry_ref, *, nseg, tiles_per_seq):
    i = pl.program_id(1)
    tm = n2_ref.shape[0]
    tn = m_ref.shape[1]
    sl = tm // nseg
    hist = SUBLANES
    sub = UP_SUB
    n2 = n2_ref[...]
    for sb in range(tn // sub):
        cs = slice(sb * sub, (sb + 1) * sub)
        w_sub = w_ref[:, 2 * sb * sub:2 * (sb + 1) * sub]
        ag = jnp.concatenate([_dot(n2[r:r + UP_DOT_ROWS, :], w_sub)
                              for r in range(0, tm, UP_DOT_ROWS)], axis=0)
        a = ag[:, :sub]
        gate = ag[:, sub:]
        w0 = cw_ref[0:1, cs]
        w1 = cw_ref[1:2, cs]
        w2 = cw_ref[2:3, cs]
        cb = cb_ref[:, cs]
        for r0 in range(0, tm, UP_ROWS):
            if r0 % sl:
                prev = a[r0 - hist:r0, :]
            elif nseg == 1:
                first = (i % tiles_per_seq) == 0
                prev = jnp.where(first, 0.0, carry_ref[:, cs])
            else:
                prev = st_ref[r0 // sl, :, cs]
            ac_rows = a[r0:r0 + UP_ROWS, :]
            win = jnp.concatenate([prev, ac_rows], axis=0)
            a_m1 = pltpu.roll(win, 1, 0)[hist:, :]
            a_m2 = pltpu.roll(win, 2, 0)[hist:, :]
            act = _silu(cb + w0 * a_m2 + w1 * a_m1 + w2 * ac_rows)
            m_ref[r0:r0 + UP_ROWS, cs] = (act * gate[r0:r0 + UP_ROWS, :]).astype(BF16)
        for s in range(nseg):
            tail = a[(s + 1) * sl - hist:(s + 1) * sl, :]
            if nseg == 1:
                carry_ref[:, cs] = tail
                cs_ref[0, :, cs] = tail
            else:
                cs_ref[s, :, cs] = tail


def _up(n2, wag, cw, cb, state8, *, nseg, tiles_per_seq):
    m = n2.shape[0]
    tm = 1024
    nt = m // tm
    tn = FF_TILE
    nj = D_FF_PAD // tn
    if nseg == 1:
        cs_rows = nt // tiles_per_seq
        cs_spec = pl.BlockSpec((1, SUBLANES, tn), lambda j, i: (i // tiles_per_seq, 0, j))
        st_spec = pl.BlockSpec((1, SUBLANES, tn), lambda j, i: (0, 0, j))
    else:
        cs_rows = nt * nseg
        cs_spec = pl.BlockSpec((nseg, SUBLANES, tn), lambda j, i: (i, 0, j))
        st_spec = pl.BlockSpec((nseg, SUBLANES, tn), lambda j, i: (i, 0, j))
    w_spec = pl.BlockSpec((D_MODEL, 2 * tn), lambda j, i: (0, j),
                          pipeline_mode=pl.Buffered(1))
    return pl.pallas_call(
        functools.partial(_up_kernel, nseg=nseg, tiles_per_seq=tiles_per_seq),
        grid=(nj, nt),
        in_specs=[pl.BlockSpec((tm, D_MODEL), lambda j, i: (i, 0)),
                  w_spec,
                  pl.BlockSpec((CONV_W, tn), lambda j, i: (0, j)),
                  pl.BlockSpec((1, tn), lambda j, i: (0, j)),
                  st_spec],
        out_specs=[pl.BlockSpec((tm, tn), lambda j, i: (i, j)), cs_spec],
        out_shape=[jax.ShapeDtypeStruct((m, D_FF_PAD), BF16),
                   jax.ShapeDtypeStruct((cs_rows, SUBLANES, D_FF_PAD), F32)],
        scratch_shapes=[pltpu.VMEM((SUBLANES, tn), F32)],
        compiler_params=pltpu.CompilerParams(
            dimension_semantics=("arbitrary", "arbitrary"),
            vmem_limit_bytes=VMEM_LIMIT_BYTES),
        name="up",
    )(n2, wag, cw, cb, state8)


def _down_kernel(m_ref, w_ref, h_ref, y_ref):
    y_ref[...] = h_ref[...] + _dot(m_ref[...], w_ref[...])


def _down(mm, w_down, h):
    m = mm.shape[0]
    tm = 512
    return pl.pallas_call(
        _down_kernel,
        grid=(m // tm,),
        in_specs=[pl.BlockSpec((tm, D_FF_PAD), lambda i: (i, 0)),
                  pl.BlockSpec((D_FF_PAD, D_MODEL), lambda i: (0, 0),
                               pipeline_mode=pl.Buffered(1)),
                  pl.BlockSpec((tm, D_MODEL), lambda i: (i, 0))],
        out_specs=pl.BlockSpec((tm, D_MODEL), lambda i: (i, 0)),
        out_shape=jax.ShapeDtypeStruct((m, D_MODEL), F32),
        compiler_params=pltpu.CompilerParams(
            dimension_semantics=("arbitrary",), vmem_limit_bytes=VMEM_LIMIT_BYTES),
        name="down",
    )(mm, w_down, h)


def _band_bias(table):
    d = jnp.arange(-(CHUNK - 1), BAND)
    e = table[:, jnp.clip(ATTN_WINDOW - d, -REL_CLIP, REL_CLIP) + REL_CLIP].astype(F32)
    n = e.shape[1]
    ep = jnp.pad(e, ((0, 0), (0, 1)))
    toep = jnp.tile(ep, (1, CHUNK))[:, :CHUNK * n].reshape(N_HEADS, CHUNK, n)
    return toep[:, :, CHUNK - 1:CHUNK - 1 + BAND] * LOG2E


def _group_bias(bias):
    rows = [jnp.pad(bias, ((0, 0), (0, 0), (c * CHUNK, GK - BAND - c * CHUNK)),
                    constant_values=NEG_INF) for c in range(ATTN_GROUP)]
    return jnp.concatenate(rows, axis=1)


def _prep_weights(w_in, w_out, w_up, cw, cb, w_down):
    pad = D_FF_PAD - D_FF
    halves = jnp.pad(w_up.reshape(D_MODEL, 2, D_FF), ((0, 0), (0, 0), (0, pad)))
    wag = jnp.swapaxes(halves.reshape(D_MODEL, 2, D_FF_PAD // UP_SUB, UP_SUB), 1, 2)
    wag = wag.reshape(D_MODEL, 2 * D_FF_PAD).astype(BF16)
    wd = jnp.pad(w_down, ((0, pad), (0, 0))).astype(BF16)
    cwp = jnp.pad(cw, ((0, 0), (0, pad)))
    cbp = jnp.pad(cb, ((0, pad),))[None, :]
    return w_in.astype(BF16), w_out.astype(BF16), wag, cwp, cbp, wd


def _layer(x2d, seq, sample_cache, nmg, w_in, qg, kg, bias, lng, lnb, ws, bs, w_out, nfg,
           wag, cw, cb, wd):
    m = x2d.shape[0]
    batch = m // seq
    is_sample = sample_cache is not None
    row = lambda v: v[None, :]
    if is_sample:
        z, kf, vf, vnf = _inproj(x2d, row(nmg), w_in, row(qg), row(kg), row(lng), row(lnb),
                                 keep_every=1, emit_vn=True)
        ck, cv, cst = sample_cache
        w_cache = ck.shape[1]
        k_all = jnp.concatenate([ck.reshape(batch, w_cache, D_ATTN).astype(BF16),
                                 z[:, D_ATTN:2 * D_ATTN].reshape(batch, seq, D_ATTN)], axis=1)
        v_all = jnp.concatenate([cv.reshape(batch, w_cache, D_ATTN).astype(BF16),
                                 z[:, 2 * D_ATTN:3 * D_ATTN].reshape(batch, seq, D_ATTN)], axis=1)
        oa = _attn_sample(z, k_all, v_all, bias[:, :, BAND - w_cache - seq:],
                          batch=batch, t=seq)
        chunk = seq
        state8 = jnp.pad(cst, ((0, 0), (SUBLANES - (CONV_W - 1), 0), (0, D_FF_PAD - D_FF)))
        nseg, tiles_per_seq = batch, 1
    else:
        keep_every = seq // 512
        z, kf, vf = _inproj(x2d, row(nmg), w_in, row(qg), row(kg), row(lng), row(lnb),
                            keep_every=keep_every, emit_vn=False)
        vnf = None
        oa = _attn_prompt(z, _group_bias(bias), batch=batch, seq=seq)
        chunk = GMLP_CHUNK
        state8 = jnp.zeros((1, SUBLANES, D_FF_PAD), F32)
        nseg, tiles_per_seq = 1, seq // 1024
    wsl = ws[:, :chunk, :chunk]
    bsb = jnp.broadcast_to(bs[:, :chunk, None], (N_GROUPS, chunk, GROUP_DIM))
    h, n2 = _outproj(oa, z, x2d, w_out, wsl, bsb, row(nfg), chunk=chunk)
    mm, cs = _up(n2, wag, cw, cb, state8, nseg=nseg, tiles_per_seq=tiles_per_seq)
    y = _down(mm, wd, h)
    conv_state = cs[:, SUBLANES - (CONV_W - 1):, :D_FF]
    return y, kf, vf, vnf, conv_state


def kernel(x_prompt, x_sample, cache_attn_k, cache_attn_v, state_ffn_conv, norm_mix_g, w_in,
           q_norm_g, k_norm_g, rel_bias_table, gmlp_ln_g, gmlp_ln_b, gmlp_w_s, gmlp_b_s, w_out,
           norm_ffn_g, w_up, ffn_conv_w, ffn_conv_b, w_down):
    batch, seq, _ = x_prompt.shape
    dbatch, dseq, _ = x_sample.shape
    depth = w_in.shape[0]
    xp = x_prompt.reshape(batch * seq, D_MODEL)
    xs = x_sample.reshape(dbatch * dseq, D_MODEL)
    keep = min(ATTN_WINDOW, seq)
    outs = [[] for _ in range(7)]
    for l in range(depth):
        w_in_b, w_out_b, wag, cw, cb, wd = _prep_weights(
            w_in[l], w_out[l], w_up[l], ffn_conv_w[l], ffn_conv_b[l], w_down[l])
        bias = _band_bias(rel_bias_table[l])
        shared = (norm_mix_g[l], w_in_b, q_norm_g[l], k_norm_g[l], bias, gmlp_ln_g[l],
                  gmlp_ln_b[l], gmlp_w_s[l], gmlp_b_s[l], w_out_b, norm_ffn_g[l],
                  wag, cw, cb, wd)
        xp, kp, vp, _, cp = _layer(xp, seq, None, *shared)
        xs, ks, vs, gs, cs = _layer(
            xs, dseq, (cache_attn_k[l], cache_attn_v[l], state_ffn_conv[l]), *shared)
        outs[0].append(kp.reshape(batch, keep, N_HEADS, HEAD_DIM))
        outs[1].append(vp.reshape(batch, keep, N_HEADS, HEAD_DIM))
        outs[2].append(cp)
        outs[3].append(ks.reshape(dbatch, dseq, N_HEADS, HEAD_DIM))
        outs[4].append(vs.reshape(dbatch, dseq, N_HEADS, HEAD_DIM))
        outs[5].append(gs.reshape(dbatch, dseq, D_GMLP))
        outs[6].append(cs)
    return (xp.reshape(batch, seq, D_MODEL), xs.reshape(dbatch, dseq, D_MODEL),
            *[jnp.stack(o) for o in outs])
```

```python
import functools

import jax
import jax.numpy as jnp
from jax import lax
from jax.experimental import pallas as pl
from jax.experimental.pallas import tpu as pltpu

D_MODEL = 2048
CHUNK = 64
ATTN_WINDOW = 8 * CHUNK
BAND = ATTN_WINDOW + CHUNK
D_ATTN = D_MODEL // 2
HEAD_DIM = 128
N_HEADS = D_ATTN // HEAD_DIM
D_GMLP = D_MODEL - D_ATTN
N_GROUPS = 8
GROUP_DIM = D_GMLP // N_GROUPS
GMLP_CHUNK = 128
REL_CLIP = 128
D_FF = 5504
CONV_W = 3
D_IN = 3 * D_ATTN + 2 * D_GMLP
EPS = 1e-6
NEG_INF = -1e30
LOG2E = 1.4426950408889634

LANES = 128
SUBLANES = 8
VMEM_LIMIT_BYTES = 60 * 1024 * 1024

FF_TILE = 2816
D_FF_PAD = 2 * FF_TILE
UP_SUB = 256
UP_ROWS = 64
UP_DOT_ROWS = 256
OUT_ROWS = 256
ATTN_GROUP = 4
GQ = ATTN_GROUP * CHUNK
GK = GQ + ATTN_WINDOW
BAND_PAD = BAND + CHUNK
assert BAND_PAD % LANES == 0 and LANES == 2 * CHUNK

F32 = jnp.float32
BF16 = jnp.bfloat16


def _dot(a, b):
    return jnp.dot(a, b, preferred_element_type=F32)


def _gelu(x):
    return 0.5 * x * (1.0 + lax.erf(x * (0.5 ** 0.5)))


def _inproj_kernel(x_ref, nmg_ref, w_ref, qg_ref, kg_ref, lng_ref, lnb_ref,
                   z_ref, kf_ref, vf_ref, *rest, emit_vn):
    x = x_ref[...]
    ms = jnp.mean(x * x, axis=-1, keepdims=True)
    n = (x * lax.rsqrt(ms + EPS) * nmg_ref[...]).astype(BF16)

    def head_rms(zh, g):
        r = lax.rsqrt(jnp.mean(zh * zh, axis=-1, keepdims=True) + EPS)
        return zh * r * g

    sub = 4 * HEAD_DIM
    order = sorted(range(D_IN // sub), key=lambda sb: (4, 3, 0, 1, 2).index(sb * sub // D_ATTN))
    for sb in order:
        seg = (sb * sub) // D_ATTN
        zs = _dot(n, w_ref[:, sb * sub:(sb + 1) * sub])
        for hh in range(sub // HEAD_DIM):
            c0 = sb * sub + hh * HEAD_DIM
            lo = c0 - seg * D_ATTN
            zh = zs[:, hh * HEAD_DIM:(hh + 1) * HEAD_DIM]
            if seg == 0:
                out = head_rms(zh, qg_ref[...] * (HEAD_DIM ** -0.5 * LOG2E))
            elif seg == 1:
                out = head_rms(zh, kg_ref[...])
            elif seg == 2:
                out = zh
            elif seg == 3:
                out = _gelu(zh)
            else:
                ge = _gelu(zh)
                mu = jnp.mean(ge, axis=-1, keepdims=True)
                d = ge - mu
                var = jnp.mean(d * d, axis=-1, keepdims=True)
                out = d * lax.rsqrt(var + EPS) * lng_ref[...] + lnb_ref[...]
            z_ref[:, c0:c0 + HEAD_DIM] = out.astype(BF16)
            if seg == 1:
                kf_ref[:, lo:lo + HEAD_DIM] = out
            elif seg == 2:
                vf_ref[:, lo:lo + HEAD_DIM] = out
            elif seg == 4 and emit_vn:
                rest[0][:, lo:lo + HEAD_DIM] = out


def _inproj(x2d, nmg, w_in, qg, kg, lng, lnb, *, keep_every, emit_vn):
    m = x2d.shape[0]
    tm = 512
    nt = m // tm
    nkeep = nt // keep_every
    vec = lambda width: pl.BlockSpec((1, width), lambda i: (0, 0))
    keep_spec = pl.BlockSpec((tm, D_ATTN), lambda i: (i // keep_every, 0))
    out_specs = [pl.BlockSpec((tm, D_IN), lambda i: (i, 0)), keep_spec, keep_spec]
    out_shape = [jax.ShapeDtypeStruct((m, D_IN), BF16),
                 jax.ShapeDtypeStruct((nkeep * tm, D_ATTN), F32),
                 jax.ShapeDtypeStruct((nkeep * tm, D_ATTN), F32)]
    if emit_vn:
        out_specs.append(pl.BlockSpec((tm, D_GMLP), lambda i: (i, 0)))
        out_shape.append(jax.ShapeDtypeStruct((m, D_GMLP), F32))
    return pl.pallas_call(
        functools.partial(_inproj_kernel, emit_vn=emit_vn),
        grid=(nt,),
        in_specs=[pl.BlockSpec((tm, D_MODEL), lambda i: (i, 0)),
                  vec(D_MODEL),
                  pl.BlockSpec((D_MODEL, D_IN), lambda i: (0, 0),
                               pipeline_mode=pl.Buffered(1)),
                  vec(HEAD_DIM), vec(HEAD_DIM), vec(GROUP_DIM), vec(GROUP_DIM)],
        out_specs=out_specs,
        out_shape=out_shape,
        compiler_params=pltpu.CompilerParams(
            dimension_semantics=("arbitrary",), vmem_limit_bytes=VMEM_LIMIT_BYTES),
        name="inproj",
    )(x2d, nmg, w_in, qg, kg, lng, lnb)


def _softmax_numer(sc):
    m = jnp.max(sc, axis=-1, keepdims=True)
    return jnp.exp2(sc - m).astype(BF16)


def _pv(p, v):
    ol = _dot(p, v)
    return ol[:, :HEAD_DIM] * (1.0 / ol[:, HEAD_DIM:])


def _qk(q, k):
    return lax.dot_general(q, k, (((1,), (1,)), ((), ())), preferred_element_type=F32)


def _attn_prompt_kernel(q_ref, k_ref, v_ref, bm_ref, o_ref, kp_ref, vp_ref):
    seq = q_ref.shape[0]
    zeros = jnp.zeros((ATTN_WINDOW, HEAD_DIM), BF16)
    kp_ref[0:ATTN_WINDOW, :] = zeros
    vp_ref[0:ATTN_WINDOW, 0:HEAD_DIM] = zeros
    kp_ref[ATTN_WINDOW:, :] = k_ref[...]
    vp_ref[ATTN_WINDOW:, 0:HEAD_DIM] = v_ref[...]
    vp_ref[:, HEAD_DIM:] = jnp.ones((seq + ATTN_WINDOW, HEAD_DIM), BF16)

    def group(q0, pre_stream):
        q = q_ref[pl.ds(q0, GQ), :]
        kb = kp_ref[pl.ds(q0, GK), :]
        vb = vp_ref[pl.ds(q0, GK), :]
        s = _qk(q, kb)
        rows = []
        for c in range(ATTN_GROUP):
            off = (c * CHUNK) // LANES * LANES
            variant = (c * CHUNK - off) // CHUNK
            sc = s[c * CHUNK:(c + 1) * CHUNK, off:off + BAND_PAD] + bm_ref[0, variant]
            if pre_stream:
                col = lax.broadcasted_iota(jnp.int32, (CHUNK, BAND_PAD), 1) + off
                sc = jnp.where(col >= ATTN_WINDOW - q0, sc, NEG_INF)
            parts = [_softmax_numer(sc)]
            if off:
                parts.insert(0, jnp.zeros((CHUNK, off), BF16))
            if GK - off - BAND_PAD:
                parts.append(jnp.zeros((CHUNK, GK - off - BAND_PAD), BF16))
            rows.append(jnp.concatenate(parts, axis=1))
        p = jnp.concatenate(rows, axis=0)
        o_ref[pl.ds(q0, GQ), :] = _pv(p, vb).astype(BF16)

    n_pre = ATTN_WINDOW // GQ
    for gi in range(n_pre):
        group(gi * GQ, True)

    def body(gi, carry):
        group(pl.multiple_of(gi * GQ, GQ), False)
        return carry

    lax.fori_loop(n_pre, seq // GQ, body, 0, unroll=True)


def _attn_prompt(z, bm, *, batch, seq):
    return pl.pallas_call(
        _attn_prompt_kernel,
        grid=(batch, N_HEADS),
        in_specs=[pl.BlockSpec((seq, HEAD_DIM), lambda b, h: (b, h)),
                  pl.BlockSpec((seq, HEAD_DIM), lambda b, h: (b, N_HEADS + h)),
                  pl.BlockSpec((seq, HEAD_DIM), lambda b, h: (b, 2 * N_HEADS + h)),
                  pl.BlockSpec((1, 2, CHUNK, BAND_PAD), lambda b, h: (h, 0, 0, 0))],
        out_specs=pl.BlockSpec((seq, HEAD_DIM), lambda b, h: (b, h)),
        out_shape=jax.ShapeDtypeStruct((batch * seq, D_ATTN), BF16),
        scratch_shapes=[pltpu.VMEM((seq + ATTN_WINDOW, HEAD_DIM), BF16),
                        pltpu.VMEM((seq + ATTN_WINDOW, 2 * HEAD_DIM), BF16)],
        compiler_params=pltpu.CompilerParams(
            dimension_semantics=("arbitrary", "arbitrary"),
            vmem_limit_bytes=VMEM_LIMIT_BYTES),
        name="attn_prompt",
    )(z, z, z, bm)


def _attn_sample_kernel(q_ref, k_ref, v_ref, bias_ref, o_ref):
    ones = jnp.ones((k_ref.shape[1], HEAD_DIM), BF16)
    for h in range(N_HEADS):
        lo = h * HEAD_DIM
        q = q_ref[:, lo:lo + HEAD_DIM]
        s = _qk(q, k_ref[0, :, lo:lo + HEAD_DIM]) + bias_ref[h]
        o = _pv(_softmax_numer(s), jnp.concatenate([v_ref[0, :, lo:lo + HEAD_DIM], ones], axis=1))
        o_ref[:, lo:lo + HEAD_DIM] = o.astype(BF16)


def _attn_sample(z, k_all, v_all, bias, *, batch, t):
    nk = k_all.shape[1]
    return pl.pallas_call(
        _attn_sample_kernel,
        grid=(batch,),
        in_specs=[pl.BlockSpec((t, D_ATTN), lambda b: (b, 0)),
                  pl.BlockSpec((1, nk, D_ATTN), lambda b: (b, 0, 0)),
                  pl.BlockSpec((1, nk, D_ATTN), lambda b: (b, 0, 0)),
                  pl.BlockSpec((N_HEADS, t, nk), lambda b: (0, 0, 0))],
        out_specs=pl.BlockSpec((t, D_ATTN), lambda b: (b, 0)),
        out_shape=jax.ShapeDtypeStruct((batch * t, D_ATTN), BF16),
        compiler_params=pltpu.CompilerParams(
            dimension_semantics=("arbitrary",), vmem_limit_bytes=VMEM_LIMIT_BYTES),
        name="attn_sample",
    )(z, k_all, v_all, bias)


def _outproj_kernel(oa_ref, u_ref, vn_ref, x_ref, wout_ref, ws_ref, bsb_ref, g_ref,
                    h_ref, n2_ref, *, chunk):
    tm = x_ref.shape[0]
    row = lax.broadcasted_iota(jnp.int32, (chunk, chunk), 0)
    col = lax.broadcasted_iota(jnp.int32, (chunk, chunk), 1)
    tri = row >= col
    wsg = [jnp.where(tri, ws_ref[g], 0.0).astype(BF16) for g in range(N_GROUPS)]
    nck = OUT_ROWS // chunk
    for r0 in range(0, tm, OUT_ROWS):
        ob = [[None] * N_GROUPS for _ in range(nck)]
        for g in range(N_GROUPS):
            gc = slice(g * GROUP_DIM, (g + 1) * GROUP_DIM)
            vn = jnp.concatenate([vn_ref[r0 + c * chunk:r0 + (c + 1) * chunk, gc]
                                  for c in range(nck)], axis=1)
            vs = _dot(wsg[g], vn)
            for c in range(nck):
                rows = slice(r0 + c * chunk, r0 + (c + 1) * chunk)
                gate = vs[:, c * GROUP_DIM:(c + 1) * GROUP_DIM] + bsb_ref[g]
                ob[c][g] = (u_ref[rows, gc].astype(F32) * gate).astype(BF16)
        ob = jnp.concatenate([jnp.concatenate(obc, axis=1) for obc in ob], axis=0)
        rows = slice(r0, r0 + OUT_ROWS)
        h = (x_ref[rows, :] + _dot(oa_ref[rows, :], wout_ref[0:D_ATTN, :])
             + _dot(ob, wout_ref[D_ATTN:, :]))
        h_ref[rows, :] = h
        ms = jnp.mean(h * h, axis=-1, keepdims=True)
        n2_ref[rows, :] = (h * lax.rsqrt(ms + EPS) * g_ref[...]).astype(BF16)


def _outproj(oa, z, x2d, w_out, ws, bsb, nfg, *, chunk):
    m = x2d.shape[0]
    tm = 512
    return pl.pallas_call(
        functools.partial(_outproj_kernel, chunk=chunk),
        grid=(m // tm,),
        in_specs=[pl.BlockSpec((tm, D_ATTN), lambda i: (i, 0)),
                  pl.BlockSpec((tm, D_GMLP), lambda i: (i, 3)),
                  pl.BlockSpec((tm, D_GMLP), lambda i: (i, 4)),
                  pl.BlockSpec((tm, D_MODEL), lambda i: (i, 0)),
                  pl.BlockSpec((D_MODEL, D_MODEL), lambda i: (0, 0),
                               pipeline_mode=pl.Buffered(1)),
                  pl.BlockSpec((N_GROUPS, chunk, chunk), lambda i: (0, 0, 0)),
                  pl.BlockSpec((N_GROUPS, chunk, GROUP_DIM), lambda i: (0, 0, 0)),
                  pl.BlockSpec((1, D_MODEL), lambda i: (0, 0))],
        out_specs=[pl.BlockSpec((tm, D_MODEL), lambda i: (i, 0)),
                   pl.BlockSpec((tm, D_MODEL), lambda i: (i, 0))],
        out_shape=[jax.ShapeDtypeStruct((m, D_MODEL), F32),
                   jax.ShapeDtypeStruct((m, D_MODEL), BF16)],
        compiler_params=pltpu.CompilerParams(
            dimension_semantics=("arbitrary",), vmem_limit_bytes=VMEM_LIMIT_BYTES),
        name="outproj",
    )(oa, z, z, x2d, w_out, ws, bsb, nfg)


def _silu(x):
    return x * (1.0 / (1.0 + jnp.exp(-x)))


def _up_kernel(n2_ref, w_ref, cw_ref, cb_ref, st_ref, m_ref, cs_ref,
               carry_ref, *, nseg, tiles_per_seq):
    i = pl.program_id(1)
    tm = n2_ref.shape[0]
    tn = m_ref.shape[1]
    sl = tm // nseg
    hist = SUBLANES
    sub = UP_SUB
    n2 = n2_ref[...]
    for sb in range(tn // sub):
        cs = slice(sb * sub, (sb + 1) * sub)
        w_sub = w_ref[:, 2 * sb * sub:2 * (sb + 1) * sub]
        ag = jnp.concatenate([_dot(n2[r:r + UP_DOT_ROWS, :], w_sub)
                              for r in range(0, tm, UP_DOT_ROWS)], axis=0)
        a = ag[:, :sub]
        gate = ag[:, sub:]
        w0 = cw_ref[0:1, cs]
        w1 = cw_ref[1:2, cs]
        w2 = cw_ref[2:3, cs]
        cb = cb_ref[:, cs]
        for r0 in range(0, tm, UP_ROWS):
            if r0 % sl:
                prev = a[r0 - hist:r0, :]
            elif nseg == 1:
                first = (i % tiles_per_seq) == 0
                prev = jnp.where(first, 0.0, carry_ref[:, cs])
            else:
                prev = st_ref[r0 // sl, :, cs]
            ac_rows = a[r0:r0 + UP_ROWS, :]
            win = jnp.concatenate([prev, ac_rows], axis=0)
            a_m1 = pltpu.roll(win, 1, 0)[hist:, :]
            a_m2 = pltpu.roll(win, 2, 0)[hist:, :]
            act = _silu(cb + w0 * a_m2 + w1 * a_m1 + w2 * ac_rows)
            m_ref[r0:r0 + UP_ROWS, cs] = (act * gate[r0:r0 + UP_ROWS, :]).astype(BF16)
        for s in range(nseg):
            tail = a[(s + 1) * sl - hist:(s + 1) * sl, :]
            if nseg == 1:
                carry_ref[:, cs] = tail
                cs_ref[0, :, cs] = tail
            else:
                cs_ref[s, :, cs] = tail


def _up(n2, wag, cw, cb, state8, *, nseg, tiles_per_seq):
    m = n2.shape[0]
    tm = 1024
    nt = m // tm
    tn = FF_TILE
    nj = D_FF_PAD // tn
    if nseg == 1:
        cs_rows = nt // tiles_per_seq
        cs_spec = pl.BlockSpec((1, SUBLANES, tn), lambda j, i: (i // tiles_per_seq, 0, j))
        st_spec = pl.BlockSpec((1, SUBLANES, tn), lambda j, i: (0, 0, j))
    else:
        cs_rows = nt * nseg
        cs_spec = pl.BlockSpec((nseg, SUBLANES, tn), lambda j, i: (i, 0, j))
        st_spec = pl.BlockSpec((nseg, SUBLANES, tn), lambda j, i: (i, 0, j))
    w_spec = pl.BlockSpec((D_MODEL, 2 * tn), lambda j, i: (0, j),
                          pipeline_mode=pl.Buffered(1))
    return pl.pallas_call(
        functools.partial(_up_kernel, nseg=nseg, tiles_per_seq=tiles_per_seq),
        grid=(nj, nt),
        in_specs=[pl.BlockSpec((tm, D_MODEL), lambda j, i: (i, 0)),
                  w_spec,
                  pl.BlockSpec((CONV_W, tn), lambda j, i: (0, j)),
                  pl.BlockSpec((1, tn), lambda j, i: (0, j)),
                  st_spec],
        out_specs=[pl.BlockSpec((tm, tn), lambda j, i: (i, j)), cs_spec],
        out_shape=[jax.ShapeDtypeStruct((m, D_FF_PAD), BF16),
                   jax.ShapeDtypeStruct((cs_rows, SUBLANES, D_FF_PAD), F32)],
        scratch_shapes=[pltpu.VMEM((SUBLANES, tn), F32)],
        compiler_params=pltpu.CompilerParams(
            dimension_semantics=("arbitrary", "arbitrary"),
            vmem_limit_bytes=VMEM_LIMIT_BYTES),
        name="up",
    )(n2, wag, cw, cb, state8)


def _down_kernel(m_ref, w_ref, h_ref, y_ref):
    y_ref[...] = h_ref[...] + _dot(m_ref[...], w_ref[...])


def _down(mm, w_down, h):
    m = mm.shape[0]
    tm = 512
    return pl.pallas_call(
        _down_kernel,
        grid=(m // tm,),
        in_specs=[pl.BlockSpec((tm, D_FF_PAD), lambda i: (i, 0)),
                  pl.BlockSpec((D_FF_PAD, D_MODEL), lambda i: (0, 0),
                               pipeline_mode=pl.Buffered(1)),
                  pl.BlockSpec((tm, D_MODEL), lambda i: (i, 0))],
        out_specs=pl.BlockSpec((tm, D_MODEL), lambda i: (i, 0)),
        out_shape=jax.ShapeDtypeStruct((m, D_MODEL), F32),
        compiler_params=pltpu.CompilerParams(
            dimension_semantics=("arbitrary",), vmem_limit_bytes=VMEM_LIMIT_BYTES),
        name="down",
    )(mm, w_down, h)


def _band_bias(table):
    d = jnp.arange(-(CHUNK - 1), BAND)
    e = table[:, jnp.clip(ATTN_WINDOW - d, -REL_CLIP, REL_CLIP) + REL_CLIP].astype(F32)
    n = e.shape[1]
    ep = jnp.pad(e, ((0, 0), (0, 1)))
    toep = jnp.tile(ep, (1, CHUNK))[:, :CHUNK * n].reshape(N_HEADS, CHUNK, n)
    return toep[:, :, CHUNK - 1:CHUNK - 1 + BAND] * LOG2E


def _group_bias(bias):
    pad = BAND_PAD - BAND
    even = jnp.pad(bias, ((0, 0), (0, 0), (0, pad)), constant_values=NEG_INF)
    odd = jnp.pad(bias, ((0, 0), (0, 0), (pad, 0)), constant_values=NEG_INF)
    return jnp.stack([even, odd], axis=1)


def _cast_up_kernel(w_ref, o_ref):
    sub = UP_SUB
    for c in range(D_FF_PAD // sub):
        for half in range(2):
            lo = c * sub
            n = min(sub, D_FF - lo)
            dst = (2 * c + half) * sub
            o_ref[:, dst:dst + n] = w_ref[:, half * D_FF + lo:half * D_FF + lo + n].astype(BF16)
            if n < sub:
                o_ref[:, dst + n:dst + sub] = jnp.zeros((w_ref.shape[0], sub - n), BF16)


def _cast_up_weight(w_up):
    tr = 128
    return pl.pallas_call(
        _cast_up_kernel,
        grid=(D_MODEL // tr,),
        in_specs=[pl.BlockSpec((tr, 2 * D_FF), lambda i: (i, 0))],
        out_specs=pl.BlockSpec((tr, 2 * D_FF_PAD), lambda i: (i, 0)),
        out_shape=jax.ShapeDtypeStruct((D_MODEL, 2 * D_FF_PAD), BF16),
        compiler_params=pltpu.CompilerParams(
            dimension_semantics=("arbitrary",), vmem_limit_bytes=VMEM_LIMIT_BYTES),
        name="cast_up",
    )(w_up)


def _prep_weights(w_in, w_out, w_up, cw, cb, w_down):
    pad = D_FF_PAD - D_FF
    wag = _cast_up_weight(w_up)
    wd = jnp.pad(w_down, ((0, pad), (0, 0))).astype(BF16)
    cwp = jnp.pad(cw, ((0, 0), (0, pad)))
    cbp = jnp.pad(cb, ((0, pad),))[None, :]
    return w_in.astype(BF16), w_out.astype(BF16), wag, cwp, cbp, wd


def _layer(x2d, seq, sample_cache, nmg, w_in, qg, kg, bias, lng, lnb, ws, bs, w_out, nfg,
           wag, cw, cb, wd):
    m = x2d.shape[0]
    batch = m // seq
    is_sample = sample_cache is not None
    row = lambda v: v[None, :]
    if is_sample:
        z, kf, vf, vnf = _inproj(x2d, row(nmg), w_in, row(qg), row(kg), row(lng), row(lnb),
                                 keep_every=1, emit_vn=True)
        ck, cv, cst = sample_cache
        w_cache = ck.shape[1]
        k_all = jnp.concatenate([ck.reshape(batch, w_cache, D_ATTN).astype(BF16),
                                 z[:, D_ATTN:2 * D_ATTN].reshape(batch, seq, D_ATTN)], axis=1)
        v_all = jnp.concatenate([cv.reshape(batch, w_cache, D_ATTN).astype(BF16),
                                 z[:, 2 * D_ATTN:3 * D_ATTN].reshape(batch, seq, D_ATTN)], axis=1)
        oa = _attn_sample(z, k_all, v_all, bias[:, :, BAND - w_cache - seq:],
                          batch=batch, t=seq)
        chunk = seq
        state8 = jnp.pad(cst, ((0, 0), (SUBLANES - (CONV_W - 1), 0), (0, D_FF_PAD - D_FF)))
        nseg, tiles_per_seq = batch, 1
    else:
        keep_every = seq // 512
        z, kf, vf = _inproj(x2d, row(nmg), w_in, row(qg), row(kg), row(lng), row(lnb),
                            keep_every=keep_every, emit_vn=False)
        vnf = None
        oa = _attn_prompt(z, _group_bias(bias), batch=batch, seq=seq)
        chunk = GMLP_CHUNK
        state8 = jnp.zeros((1, SUBLANES, D_FF_PAD), F32)
        nseg, tiles_per_seq = 1, seq // 1024
    wsl = ws[:, :chunk, :chunk]
    bsb = jnp.broadcast_to(bs[:, :chunk, None], (N_GROUPS, chunk, GROUP_DIM))
    h, n2 = _outproj(oa, z, x2d, w_out, wsl, bsb, row(nfg), chunk=chunk)
    mm, cs = _up(n2, wag, cw, cb, state8, nseg=nseg, tiles_per_seq=tiles_per_seq)
    y = _down(mm, wd, h)
    conv_state = cs[:, SUBLANES - (CONV_W - 1):, :D_FF]
    return y, kf, vf, vnf, conv_state


def kernel(x_prompt, x_sample, cache_attn_k, cache_attn_v, state_ffn_conv, norm_mix_g, w_in,
           q_norm_g, k_norm_g, rel_bias_table, gmlp_ln_g, gmlp_ln_b, gmlp_w_s, gmlp_b_s, w_out,
           norm_ffn_g, w_up, ffn_conv_w, ffn_conv_b, w_down):
    batch, seq, _ = x_prompt.shape
    dbatch, dseq, _ = x_sample.shape
    depth = w_in.shape[0]
    xp = x_prompt.reshape(batch * seq, D_MODEL)
    xs = x_sample.reshape(dbatch * dseq, D_MODEL)
    keep = min(ATTN_WINDOW, seq)
    outs = [[] for _ in range(7)]
    for l in range(depth):
        w_in_b, w_out_b, wag, cw, cb, wd = _prep_weights(
            w_in[l], w_out[l], w_up[l], ffn_conv_w[l], ffn_conv_b[l], w_down[l])
        bias = _band_bias(rel_bias_table[l])
        shared = (norm_mix_g[l], w_in_b, q_norm_g[l], k_norm_g[l], bias, gmlp_ln_g[l],
                  gmlp_ln_b[l], gmlp_w_s[l], gmlp_b_s[l], w_out_b, norm_ffn_g[l],
                  wag, cw, cb, wd)
        xp, kp, vp, _, cp = _layer(xp, seq, None, *shared)
        xs, ks, vs, gs, cs = _layer(
            xs, dseq, (cache_attn_k[l], cache_attn_v[l], state_ffn_conv[l]), *shared)
        outs[0].append(kp.reshape(batch, keep, N_HEADS, HEAD_DIM))
        outs[1].append(vp.reshape(batch, keep, N_HEADS, HEAD_DIM))
        outs[2].append(cp)
        outs[3].append(ks.reshape(dbatch, dseq, N_HEADS, HEAD_DIM))
        outs[4].append(vs.reshape(dbatch, dseq, N_HEADS, HEAD_DIM))
        outs[5].append(gs.reshape(dbatch, dseq, D_GMLP))
        outs[6].append(cs)
    return (xp.reshape(batch, seq, D_MODEL), xs.reshape(dbatch, dseq, D_MODEL),
            *[jnp.stack(o) for o in outs])
```

```python
import functools

import jax
import jax.numpy as jnp
from jax import lax
from jax.experimental import pallas as pl
from jax.experimental.pallas import tpu as pltpu

D_MODEL = 2048
CHUNK = 64
ATTN_WINDOW = 8 * CHUNK
BAND = ATTN_WINDOW + CHUNK
D_ATTN = D_MODEL // 2
HEAD_DIM = 128
N_HEADS = D_ATTN // HEAD_DIM
D_GMLP = D_MODEL - D_ATTN
N_GROUPS = 8
GROUP_DIM = D_GMLP // N_GROUPS
GMLP_CHUNK = 128
REL_CLIP = 128
D_FF = 5504
CONV_W = 3
D_IN = 3 * D_ATTN + 2 * D_GMLP
EPS = 1e-6
NEG_INF = -1e30
LOG2E = 1.4426950408889634

LANES = 128
SUBLANES = 8
VMEM_LIMIT_BYTES = 60 * 1024 * 1024

FF_TILE = 2816
D_FF_PAD = 2 * FF_TILE
UP_SUB = 256
UP_ROWS = 64
UP_DOT_ROWS = 256
OUT_ROWS = 256
ATTN_GROUP = 4
GQ = ATTN_GROUP * CHUNK
GK = GQ + ATTN_WINDOW
BAND_PAD = BAND + CHUNK
assert BAND_PAD % LANES == 0 and LANES == 2 * CHUNK

F32 = jnp.float32
BF16 = jnp.bfloat16


def _dot(a, b):
    return jnp.dot(a, b, preferred_element_type=F32)


def _gelu(x):
    return 0.5 * x * (1.0 + lax.erf(x * (0.5 ** 0.5)))


def _inproj_kernel(x_ref, nmg_ref, w_ref, qg_ref, kg_ref, lng_ref, lnb_ref,
                   z_ref, kf_ref, vf_ref, *rest, emit_vn):
    x = x_ref[...]
    ms = jnp.mean(x * x, axis=-1, keepdims=True)
    n = (x * lax.rsqrt(ms + EPS) * nmg_ref[...]).astype(BF16)

    def head_rms(zh, g):
        r = lax.rsqrt(jnp.mean(zh * zh, axis=-1, keepdims=True) + EPS)
        return zh * r * g

    sub = 4 * HEAD_DIM
    order = sorted(range(D_IN // sub), key=lambda sb: (4, 3, 0, 1, 2).index(sb * sub // D_ATTN))
    for sb in order:
        seg = (sb * sub) // D_ATTN
        zs = _dot(n, w_ref[:, sb * sub:(sb + 1) * sub])
        for hh in range(sub // HEAD_DIM):
            c0 = sb * sub + hh * HEAD_DIM
            lo = c0 - seg * D_ATTN
            zh = zs[:, hh * HEAD_DIM:(hh + 1) * HEAD_DIM]
            if seg == 0:
                out = head_rms(zh, qg_ref[...] * (HEAD_DIM ** -0.5 * LOG2E))
            elif seg == 1:
                out = head_rms(zh, kg_ref[...])
            elif seg == 2:
                out = zh
            elif seg == 3:
                out = _gelu(zh)
            else:
                ge = _gelu(zh)
                mu = jnp.mean(ge, axis=-1, keepdims=True)
                d = ge - mu
                var = jnp.mean(d * d, axis=-1, keepdims=True)
                out = d * lax.rsqrt(var + EPS) * lng_ref[...] + lnb_ref[...]
            z_ref[:, c0:c0 + HEAD_DIM] = out.astype(BF16)
            head_rows = pl.ds(lo // HEAD_DIM, zh.shape[0], stride=N_HEADS)
            if seg == 1:
                kf_ref[head_rows, :] = out
            elif seg == 2:
                vf_ref[head_rows, :] = out
            elif seg == 4 and emit_vn:
                rest[0][:, lo:lo + HEAD_DIM] = out


def _inproj(x2d, nmg, w_in, qg, kg, lng, lnb, *, keep_every, emit_vn):
    m = x2d.shape[0]
    tm = 512
    nt = m // tm
    nkeep = nt // keep_every
    vec = lambda width: pl.BlockSpec((1, width), lambda i: (0, 0))
    keep_spec = pl.BlockSpec((tm * N_HEADS, HEAD_DIM), lambda i: (i // keep_every, 0))
    out_specs = [pl.BlockSpec((tm, D_IN), lambda i: (i, 0)), keep_spec, keep_spec]
    out_shape = [jax.ShapeDtypeStruct((m, D_IN), BF16),
                 jax.ShapeDtypeStruct((nkeep * tm * N_HEADS, HEAD_DIM), F32),
                 jax.ShapeDtypeStruct((nkeep * tm * N_HEADS, HEAD_DIM), F32)]
    if emit_vn:
        out_specs.append(pl.BlockSpec((tm, D_GMLP), lambda i: (i, 0)))
        out_shape.append(jax.ShapeDtypeStruct((m, D_GMLP), F32))
    return pl.pallas_call(
        functools.partial(_inproj_kernel, emit_vn=emit_vn),
        grid=(nt,),
        in_specs=[pl.BlockSpec((tm, D_MODEL), lambda i: (i, 0)),
                  vec(D_MODEL),
                  pl.BlockSpec((D_MODEL, D_IN), lambda i: (0, 0),
                               pipeline_mode=pl.Buffered(1)),
                  vec(HEAD_DIM), vec(HEAD_DIM), vec(GROUP_DIM), vec(GROUP_DIM)],
        out_specs=out_specs,
        out_shape=out_shape,
        compiler_params=pltpu.CompilerParams(
            dimension_semantics=("arbitrary",), vmem_limit_bytes=VMEM_LIMIT_BYTES),
        name="inproj",
    )(x2d, nmg, w_in, qg, kg, lng, lnb)


def _softmax_numer(sc):
    m = jnp.max(sc, axis=-1, keepdims=True)
    return jnp.exp2(sc - m).astype(BF16)


def _pv(p, v):
    ol = _dot(p, v)
    return ol[:, :HEAD_DIM] * (1.0 / ol[:, HEAD_DIM:])


def _qk(q, k):
    return lax.dot_general(q, k, (((1,), (1,)), ((), ())), preferred_element_type=F32)


def _attn_prompt_kernel(q_ref, k_ref, v_ref, bm_ref, o_ref, kp_ref, vp_ref):
    seq = q_ref.shape[0]
    zeros = jnp.zeros((ATTN_WINDOW, HEAD_DIM), BF16)
    kp_ref[0:ATTN_WINDOW, :] = zeros
    vp_ref[0:ATTN_WINDOW, 0:HEAD_DIM] = zeros
    kp_ref[ATTN_WINDOW:, :] = k_ref[...]
    vp_ref[ATTN_WINDOW:, 0:HEAD_DIM] = v_ref[...]
    vp_ref[:, HEAD_DIM:] = jnp.ones((seq + ATTN_WINDOW, HEAD_DIM), BF16)

    def group(q0, pre_stream):
        q = q_ref[pl.ds(q0, GQ), :]
        kb = kp_ref[pl.ds(q0, GK), :]
        vb = vp_ref[pl.ds(q0, GK), :]
        s = _qk(q, kb)
        rows = []
        for c in range(ATTN_GROUP):
            off = (c * CHUNK) // LANES * LANES
            variant = (c * CHUNK - off) // CHUNK
            sc = s[c * CHUNK:(c + 1) * CHUNK, off:off + BAND_PAD] + bm_ref[0, variant]
            if pre_stream:
                col = lax.broadcasted_iota(jnp.int32, (CHUNK, BAND_PAD), 1) + off
                sc = jnp.where(col >= ATTN_WINDOW - q0, sc, NEG_INF)
            parts = [_softmax_numer(sc)]
            if off:
                parts.insert(0, jnp.zeros((CHUNK, off), BF16))
            if GK - off - BAND_PAD:
                parts.append(jnp.zeros((CHUNK, GK - off - BAND_PAD), BF16))
            rows.append(jnp.concatenate(parts, axis=1))
        p = jnp.concatenate(rows, axis=0)
        o_ref[pl.ds(q0, GQ), :] = _pv(p, vb).astype(BF16)

    n_pre = ATTN_WINDOW // GQ
    for gi in range(n_pre):
        group(gi * GQ, True)

    def body(gi, carry):
        group(pl.multiple_of(gi * GQ, GQ), False)
        return carry

    lax.fori_loop(n_pre, seq // GQ, body, 0, unroll=True)


def _attn_prompt(z, bm, *, batch, seq):
    return pl.pallas_call(
        _attn_prompt_kernel,
        grid=(batch, N_HEADS),
        in_specs=[pl.BlockSpec((seq, HEAD_DIM), lambda b, h: (b, h)),
                  pl.BlockSpec((seq, HEAD_DIM), lambda b, h: (b, N_HEADS + h)),
                  pl.BlockSpec((seq, HEAD_DIM), lambda b, h: (b, 2 * N_HEADS + h)),
                  pl.BlockSpec((1, 2, CHUNK, BAND_PAD), lambda b, h: (h, 0, 0, 0))],
        out_specs=pl.BlockSpec((seq, HEAD_DIM), lambda b, h: (b, h)),
        out_shape=jax.ShapeDtypeStruct((batch * seq, D_ATTN), BF16),
        scratch_shapes=[pltpu.VMEM((seq + ATTN_WINDOW, HEAD_DIM), BF16),
                        pltpu.VMEM((seq + ATTN_WINDOW, 2 * HEAD_DIM), BF16)],
        compiler_params=pltpu.CompilerParams(
            dimension_semantics=("arbitrary", "arbitrary"),
            vmem_limit_bytes=VMEM_LIMIT_BYTES),
        name="attn_prompt",
    )(z, z, z, bm)


def _attn_sample_kernel(q_ref, kn_ref, vn_ref, kc_ref, vc_ref, bias_ref, o_ref):
    w = kc_ref.shape[1]
    ones = jnp.ones((w + kn_ref.shape[0], HEAD_DIM), BF16)
    for h in range(N_HEADS):
        hc = slice(h * HEAD_DIM, (h + 1) * HEAD_DIM)
        q = q_ref[:, hc]
        s = jnp.concatenate([_qk(q, kc_ref[0, :, hc]), _qk(q, kn_ref[:, hc])], axis=1)
        v = jnp.concatenate([vc_ref[0, :, hc], vn_ref[:, hc]], axis=0)
        o = _pv(_softmax_numer(s + bias_ref[h]), jnp.concatenate([v, ones], axis=1))
        o_ref[:, hc] = o.astype(BF16)


def _attn_sample(z, k_cache, v_cache, bias, *, batch, t):
    w = k_cache.shape[1]
    return pl.pallas_call(
        _attn_sample_kernel,
        grid=(batch,),
        in_specs=[pl.BlockSpec((t, D_ATTN), lambda b: (b, 0)),
                  pl.BlockSpec((t, D_ATTN), lambda b: (b, 1)),
                  pl.BlockSpec((t, D_ATTN), lambda b: (b, 2)),
                  pl.BlockSpec((1, w, D_ATTN), lambda b: (b, 0, 0)),
                  pl.BlockSpec((1, w, D_ATTN), lambda b: (b, 0, 0)),
                  pl.BlockSpec((N_HEADS, t, w + t), lambda b: (0, 0, 0))],
        out_specs=pl.BlockSpec((t, D_ATTN), lambda b: (b, 0)),
        out_shape=jax.ShapeDtypeStruct((batch * t, D_ATTN), BF16),
        compiler_params=pltpu.CompilerParams(
            dimension_semantics=("arbitrary",), vmem_limit_bytes=VMEM_LIMIT_BYTES),
        name="attn_sample",
    )(z, z, z, k_cache, v_cache, bias)


def _outproj_kernel(oa_ref, u_ref, vn_ref, x_ref, wout_ref, ws_ref, bsb_ref, g_ref,
                    h_ref, n2_ref, *, chunk):
    tm = x_ref.shape[0]
    row = lax.broadcasted_iota(jnp.int32, (chunk, chunk), 0)
    col = lax.broadcasted_iota(jnp.int32, (chunk, chunk), 1)
    tri = row >= col
    wsg = [jnp.where(tri, ws_ref[g], 0.0).astype(BF16) for g in range(N_GROUPS)]
    nck = OUT_ROWS // chunk
    for r0 in range(0, tm, OUT_ROWS):
        ob = [[None] * N_GROUPS for _ in range(nck)]
        for g in range(N_GROUPS):
            gc = slice(g * GROUP_DIM, (g + 1) * GROUP_DIM)
            vn = jnp.concatenate([vn_ref[r0 + c * chunk:r0 + (c + 1) * chunk, gc]
                                  for c in range(nck)], axis=1)
            vs = _dot(wsg[g], vn)
            for c in range(nck):
                rows = slice(r0 + c * chunk, r0 + (c + 1) * chunk)
                gate = vs[:, c * GROUP_DIM:(c + 1) * GROUP_DIM] + bsb_ref[g]
                ob[c][g] = (u_ref[rows, gc].astype(F32) * gate).astype(BF16)
        ob = jnp.concatenate([jnp.concatenate(obc, axis=1) for obc in ob], axis=0)
        rows = slice(r0, r0 + OUT_ROWS)
        h = (x_ref[rows, :] + _dot(oa_ref[rows, :], wout_ref[0:D_ATTN, :])
             + _dot(ob, wout_ref[D_ATTN:, :]))
        h_ref[rows, :] = h
        ms = jnp.mean(h * h, axis=-1, keepdims=True)
        n2_ref[rows, :] = (h * lax.rsqrt(ms + EPS) * g_ref[...]).astype(BF16)


def _outproj(oa, z, x2d, w_out, ws, bsb, nfg, *, chunk):
    m = x2d.shape[0]
    tm = 512
    return pl.pallas_call(
        functools.partial(_outproj_kernel, chunk=chunk),
        grid=(m // tm,),
        in_specs=[pl.BlockSpec((tm, D_ATTN), lambda i: (i, 0)),
                  pl.BlockSpec((tm, D_GMLP), lambda i: (i, 3)),
                  pl.BlockSpec((tm, D_GMLP), lambda i: (i, 4)),
                  pl.BlockSpec((tm, D_MODEL), lambda i: (i, 0)),
                  pl.BlockSpec((D_MODEL, D_MODEL), lambda i: (0, 0),
                               pipeline_mode=pl.Buffered(1)),
                  pl.BlockSpec((N_GROUPS, chunk, chunk), lambda i: (0, 0, 0)),
                  pl.BlockSpec((N_GROUPS, chunk, GROUP_DIM), lambda i: (0, 0, 0)),
                  pl.BlockSpec((1, D_MODEL), lambda i: (0, 0))],
        out_specs=[pl.BlockSpec((tm, D_MODEL), lambda i: (i, 0)),
                   pl.BlockSpec((tm, D_MODEL), lambda i: (i, 0))],
        out_shape=[jax.ShapeDtypeStruct((m, D_MODEL), F32),
                   jax.ShapeDtypeStruct((m, D_MODEL), BF16)],
        compiler_params=pltpu.CompilerParams(
            dimension_semantics=("arbitrary",), vmem_limit_bytes=VMEM_LIMIT_BYTES),
        name="outproj",
    )(oa, z, z, x2d, w_out, ws, bsb, nfg)


def _silu(x):
    return x * (1.0 / (1.0 + jnp.exp(-x)))


def _up_kernel(n2_ref, w_ref, cw_ref, cb_ref, st_ref, m_ref, cs_ref,
               carry_ref, *, nseg, tiles_per_seq):
    i = pl.program_id(1)
    tm = n2_ref.shape[0]
    tn = m_ref.shape[1]
    sl = tm // nseg
    hist = SUBLANES
    sub = UP_SUB
    n2 = n2_ref[...]
    for sb in range(tn // sub):
        cs = slice(sb * sub, (sb + 1) * sub)
        w_sub = w_ref[:, 2 * sb * sub:2 * (sb + 1) * sub]
        ag = jnp.concatenate([_dot(n2[r:r + UP_DOT_ROWS, :], w_sub)
                              for r in range(0, tm, UP_DOT_ROWS)], axis=0)
        a = ag[:, :sub]
        gate = ag[:, sub:]
        w0 = cw_ref[0:1, cs]
        w1 = cw_ref[1:2, cs]
        w2 = cw_ref[2:3, cs]
        cb = cb_ref[:, cs]
        for r0 in range(0, tm, UP_ROWS):
            if r0 % sl:
                prev = a[r0 - hist:r0, :]
            elif nseg == 1:
                first = (i % tiles_per_seq) == 0
                prev = jnp.where(first, 0.0, carry_ref[:, cs])
            else:
                prev = st_ref[r0 // sl, :, cs]
            ac_rows = a[r0:r0 + UP_ROWS, :]
            win = jnp.concatenate([prev, ac_rows], axis=0)
            a_m1 = pltpu.roll(win, 1, 0)[hist:, :]
            a_m2 = pltpu.roll(win, 2, 0)[hist:, :]
            act = _silu(cb + w0 * a_m2 + w1 * a_m1 + w2 * ac_rows)
            m_ref[r0:r0 + UP_ROWS, cs] = (act * gate[r0:r0 + UP_ROWS, :]).astype(BF16)
        for s in range(nseg):
            tail = a[(s + 1) * sl - hist:(s + 1) * sl, :]
            if nseg == 1:
                carry_ref[:, cs] = tail
                cs_ref[0, :, cs] = tail
            else:
                cs_ref[s, :, cs] = tail


def _up(n2, wag, cw, cb, state8, *, nseg, tiles_per_seq):
    m = n2.shape[0]
    tm = 1024
    nt = m // tm
    tn = FF_TILE
    nj = D_FF_PAD // tn
    if nseg == 1:
        cs_rows = nt // tiles_per_seq
        cs_spec = pl.BlockSpec((1, SUBLANES, tn), lambda j, i: (i // tiles_per_seq, 0, j))
        st_spec = pl.BlockSpec((1, SUBLANES, tn), lambda j, i: (0, 0, j))
    else:
        cs_rows = nt * nseg
        cs_spec = pl.BlockSpec((nseg, SUBLANES, tn), lambda j, i: (i, 0, j))
        st_spec = pl.BlockSpec((nseg, SUBLANES, tn), lambda j, i: (i, 0, j))
    w_spec = pl.BlockSpec((D_MODEL, 2 * tn), lambda j, i: (0, j),
                          pipeline_mode=pl.Buffered(1))
    return pl.pallas_call(
        functools.partial(_up_kernel, nseg=nseg, tiles_per_seq=tiles_per_seq),
        grid=(nj, nt),
        in_specs=[pl.BlockSpec((tm, D_MODEL), lambda j, i: (i, 0)),
                  w_spec,
                  pl.BlockSpec((CONV_W, tn), lambda j, i: (0, j)),
                  pl.BlockSpec((1, tn), lambda j, i: (0, j)),
                  st_spec],
        out_specs=[pl.BlockSpec((tm, tn), lambda j, i: (i, j)), cs_spec],
        out_shape=[jax.ShapeDtypeStruct((m, D_FF_PAD), BF16),
                   jax.ShapeDtypeStruct((cs_rows, SUBLANES, D_FF_PAD), F32)],
        scratch_shapes=[pltpu.VMEM((SUBLANES, tn), F32)],
        compiler_params=pltpu.CompilerParams(
            dimension_semantics=("arbitrary", "arbitrary"),
            vmem_limit_bytes=VMEM_LIMIT_BYTES),
        name="up",
    )(n2, wag, cw, cb, state8)


def _down_kernel(m_ref, w_ref, h_ref, y_ref):
    y_ref[...] = h_ref[...] + _dot(m_ref[...], w_ref[...])


def _down(mm, w_down, h):
    m = mm.shape[0]
    tm = 512
    return pl.pallas_call(
        _down_kernel,
        grid=(m // tm,),
        in_specs=[pl.BlockSpec((tm, D_FF_PAD), lambda i: (i, 0)),
                  pl.BlockSpec((D_FF_PAD, D_MODEL), lambda i: (0, 0),
                               pipeline_mode=pl.Buffered(1)),
                  pl.BlockSpec((tm, D_MODEL), lambda i: (i, 0))],
        out_specs=pl.BlockSpec((tm, D_MODEL), lambda i: (i, 0)),
        out_shape=jax.ShapeDtypeStruct((m, D_MODEL), F32),
        compiler_params=pltpu.CompilerParams(
            dimension_semantics=("arbitrary",), vmem_limit_bytes=VMEM_LIMIT_BYTES),
        name="down",
    )(mm, w_down, h)


def _band_bias(table):
    d = jnp.arange(-(CHUNK - 1), BAND)
    e = table[:, jnp.clip(ATTN_WINDOW - d, -REL_CLIP, REL_CLIP) + REL_CLIP].astype(F32)
    n = e.shape[1]
    ep = jnp.pad(e, ((0, 0), (0, 1)))
    toep = jnp.tile(ep, (1, CHUNK))[:, :CHUNK * n].reshape(N_HEADS, CHUNK, n)
    return toep[:, :, CHUNK - 1:CHUNK - 1 + BAND] * LOG2E


def _group_bias(bias):
    pad = BAND_PAD - BAND
    even = jnp.pad(bias, ((0, 0), (0, 0), (0, pad)), constant_values=NEG_INF)
    odd = jnp.pad(bias, ((0, 0), (0, 0), (pad, 0)), constant_values=NEG_INF)
    return jnp.stack([even, odd], axis=1)


def _cast_up_kernel(w_ref, o_ref):
    sub = UP_SUB
    for c in range(D_FF_PAD // sub):
        for half in range(2):
            lo = c * sub
            n = min(sub, D_FF - lo)
            dst = (2 * c + half) * sub
            o_ref[:, dst:dst + n] = w_ref[:, half * D_FF + lo:half * D_FF + lo + n].astype(BF16)
            if n < sub:
                o_ref[:, dst + n:dst + sub] = jnp.zeros((w_ref.shape[0], sub - n), BF16)


def _cast_up_weight(w_up):
    tr = 128
    return pl.pallas_call(
        _cast_up_kernel,
        grid=(D_MODEL // tr,),
        in_specs=[pl.BlockSpec((tr, 2 * D_FF), lambda i: (i, 0))],
        out_specs=pl.BlockSpec((tr, 2 * D_FF_PAD), lambda i: (i, 0)),
        out_shape=jax.ShapeDtypeStruct((D_MODEL, 2 * D_FF_PAD), BF16),
        compiler_params=pltpu.CompilerParams(
            dimension_semantics=("arbitrary",), vmem_limit_bytes=VMEM_LIMIT_BYTES),
        name="cast_up",
    )(w_up)


def _prep_weights(w_in, w_out, w_up, cw, cb, w_down):
    pad = D_FF_PAD - D_FF
    wag = _cast_up_weight(w_up)
    wd = jnp.pad(w_down, ((0, pad), (0, 0))).astype(BF16)
    cwp = jnp.pad(cw, ((0, 0), (0, pad)))
    cbp = jnp.pad(cb, ((0, pad),))[None, :]
    return w_in.astype(BF16), w_out.astype(BF16), wag, cwp, cbp, wd


def _layer(x2d, seq, sample_cache, nmg, w_in, qg, kg, bias, lng, lnb, ws, bs, w_out, nfg,
           wag, cw, cb, wd):
    m = x2d.shape[0]
    batch = m // seq
    is_sample = sample_cache is not None
    row = lambda v: v[None, :]
    if is_sample:
        z, kf, vf, vnf = _inproj(x2d, row(nmg), w_in, row(qg), row(kg), row(lng), row(lnb),
                                 keep_every=1, emit_vn=True)
        ck, cv, cst = sample_cache
        w_cache = ck.shape[1]
        oa = _attn_sample(z, ck.reshape(batch, w_cache, D_ATTN).astype(BF16),
                          cv.reshape(batch, w_cache, D_ATTN).astype(BF16),
                          bias[:, :, BAND - w_cache - seq:], batch=batch, t=seq)
        chunk = seq
        state8 = jnp.pad(cst, ((0, 0), (SUBLANES - (CONV_W - 1), 0), (0, D_FF_PAD - D_FF)))
        nseg, tiles_per_seq = batch, 1
    else:
        keep_every = seq // 512
        z, kf, vf = _inproj(x2d, row(nmg), w_in, row(qg), row(kg), row(lng), row(lnb),
                            keep_every=keep_every, emit_vn=False)
        vnf = None
        oa = _attn_prompt(z, _group_bias(bias), batch=batch, seq=seq)
        chunk = GMLP_CHUNK
        state8 = jnp.zeros((1, SUBLANES, D_FF_PAD), F32)
        nseg, tiles_per_seq = 1, seq // 1024
    wsl = ws[:, :chunk, :chunk]
    bsb = jnp.broadcast_to(bs[:, :chunk, None], (N_GROUPS, chunk, GROUP_DIM))
    h, n2 = _outproj(oa, z, x2d, w_out, wsl, bsb, row(nfg), chunk=chunk)
    mm, cs = _up(n2, wag, cw, cb, state8, nseg=nseg, tiles_per_seq=tiles_per_seq)
    y = _down(mm, wd, h)
    conv_state = cs[:, SUBLANES - (CONV_W - 1):, :D_FF]
    return y, kf, vf, vnf, conv_state


def kernel(x_prompt, x_sample, cache_attn_k, cache_attn_v, state_ffn_conv, norm_mix_g, w_in,
           q_norm_g, k_norm_g, rel_bias_table, gmlp_ln_g, gmlp_ln_b, gmlp_w_s, gmlp_b_s, w_out,
           norm_ffn_g, w_up, ffn_conv_w, ffn_conv_b, w_down):
    batch, seq, _ = x_prompt.shape
    dbatch, dseq, _ = x_sample.shape
    depth = w_in.shape[0]
    xp = x_prompt.reshape(batch * seq, D_MODEL)
    xs = x_sample.reshape(dbatch * dseq, D_MODEL)
    keep = min(ATTN_WINDOW, seq)
    outs = [[] for _ in range(7)]
    for l in range(depth):
        w_in_b, w_out_b, wag, cw, cb, wd = _prep_weights(
            w_in[l], w_out[l], w_up[l], ffn_conv_w[l], ffn_conv_b[l], w_down[l])
        bias = _band_bias(rel_bias_table[l])
        shared = (norm_mix_g[l], w_in_b, q_norm_g[l], k_norm_g[l], bias, gmlp_ln_g[l],
                  gmlp_ln_b[l], gmlp_w_s[l], gmlp_b_s[l], w_out_b, norm_ffn_g[l],
                  wag, cw, cb, wd)
        xp, kp, vp, _, cp = _layer(xp, seq, None, *shared)
        xs, ks, vs, gs, cs = _layer(
            xs, dseq, (cache_attn_k[l], cache_attn_v[l], state_ffn_conv[l]), *shared)
        outs[0].append(kp.reshape(batch, keep, N_HEADS, HEAD_DIM))
        outs[1].append(vp.reshape(batch, keep, N_HEADS, HEAD_DIM))
        outs[2].append(cp)
        outs[3].append(ks.reshape(dbatch, dseq, N_HEADS, HEAD_DIM))
        outs[4].append(vs.reshape(dbatch, dseq, N_HEADS, HEAD_DIM))
        outs[5].append(gs.reshape(dbatch, dseq, D_GMLP))
        outs[6].append(cs)
    return (xp.reshape(batch, seq, D_MODEL), xs.reshape(dbatch, dseq, D_MODEL),
            *[jnp.stack(o) for o in outs])
```

```python
import functools

import jax
import jax.numpy as jnp
from jax import lax
from jax.experimental import pallas as pl
from jax.experimental.pallas import tpu as pltpu

D_MODEL = 2048
CHUNK = 64
ATTN_WINDOW = 8 * CHUNK
BAND = ATTN_WINDOW + CHUNK
D_ATTN = D_MODEL // 2
HEAD_DIM = 128
N_HEADS = D_ATTN // HEAD_DIM
D_GMLP = D_MODEL - D_ATTN
N_GROUPS = 8
GROUP_DIM = D_GMLP // N_GROUPS
GMLP_CHUNK = 128
REL_CLIP = 128
D_FF = 5504
CONV_W = 3
D_IN = 3 * D_ATTN + 2 * D_GMLP
EPS = 1e-6
NEG_INF = -1e30
LOG2E = 1.4426950408889634

LANES = 128
SUBLANES = 8
VMEM_LIMIT_BYTES = 60 * 1024 * 1024

FF_TILE = 2816
D_FF_PAD = 2 * FF_TILE
UP_SUB = 256
UP_ROWS = 64
UP_DOT_ROWS = 256
OUT_ROWS = 256
IN_DOT_ROWS = 256
ATTN_GROUP = 4
GQ = ATTN_GROUP * CHUNK
GK = GQ + ATTN_WINDOW
BAND_PAD = BAND + CHUNK
assert BAND_PAD % LANES == 0 and LANES == 2 * CHUNK

F32 = jnp.float32
BF16 = jnp.bfloat16


def _dot(a, b):
    return jnp.dot(a, b, preferred_element_type=F32)


def _gelu(x):
    return 0.5 * x * (1.0 + lax.erf(x * (0.5 ** 0.5)))


def _inproj_kernel(x_ref, nmg_ref, w_ref, qg_ref, kg_ref, lng_ref, lnb_ref,
                   z_ref, kf_ref, vf_ref, *rest, emit_vn):
    x = x_ref[...]
    ms = jnp.mean(x * x, axis=-1, keepdims=True)
    n = (x * lax.rsqrt(ms + EPS) * nmg_ref[...]).astype(BF16)

    def head_rms(zh, g):
        r = lax.rsqrt(jnp.mean(zh * zh, axis=-1, keepdims=True) + EPS)
        return zh * r * g

    sub = 4 * HEAD_DIM
    order = sorted(range(D_IN // sub), key=lambda sb: (4, 3, 0, 1, 2).index(sb * sub // D_ATTN))
    for sb in order:
        seg = (sb * sub) // D_ATTN
        w_sub = w_ref[:, sb * sub:(sb + 1) * sub]
        zs = jnp.concatenate([_dot(n[r:r + IN_DOT_ROWS, :], w_sub)
                              for r in range(0, n.shape[0], IN_DOT_ROWS)], axis=0)
        for hh in range(sub // HEAD_DIM):
            c0 = sb * sub + hh * HEAD_DIM
            lo = c0 - seg * D_ATTN
            zh = zs[:, hh * HEAD_DIM:(hh + 1) * HEAD_DIM]
            if seg == 0:
                out = head_rms(zh, qg_ref[...] * (HEAD_DIM ** -0.5 * LOG2E))
            elif seg == 1:
                out = head_rms(zh, kg_ref[...])
            elif seg == 2:
                out = zh
            elif seg == 3:
                out = _gelu(zh)
            else:
                ge = _gelu(zh)
                mu = jnp.mean(ge, axis=-1, keepdims=True)
                d = ge - mu
                var = jnp.mean(d * d, axis=-1, keepdims=True)
                out = d * lax.rsqrt(var + EPS) * lng_ref[...] + lnb_ref[...]
            z_ref[:, c0:c0 + HEAD_DIM] = out.astype(BF16)
            head_rows = pl.ds(lo // HEAD_DIM, zh.shape[0], stride=N_HEADS)
            if seg == 1:
                kf_ref[head_rows, :] = out
            elif seg == 2:
                vf_ref[head_rows, :] = out
            elif seg == 4 and emit_vn:
                rest[0][:, lo:lo + HEAD_DIM] = out


def _inproj(x2d, nmg, w_in, qg, kg, lng, lnb, *, keep_every, emit_vn):
    m = x2d.shape[0]
    tm = 512
    nt = m // tm
    nkeep = nt // keep_every
    vec = lambda width: pl.BlockSpec((1, width), lambda i: (0, 0))
    keep_spec = pl.BlockSpec((tm * N_HEADS, HEAD_DIM), lambda i: (i // keep_every, 0))
    out_specs = [pl.BlockSpec((tm, D_IN), lambda i: (i, 0)), keep_spec, keep_spec]
    out_shape = [jax.ShapeDtypeStruct((m, D_IN), BF16),
                 jax.ShapeDtypeStruct((nkeep * tm * N_HEADS, HEAD_DIM), F32),
                 jax.ShapeDtypeStruct((nkeep * tm * N_HEADS, HEAD_DIM), F32)]
    if emit_vn:
        out_specs.append(pl.BlockSpec((tm, D_GMLP), lambda i: (i, 0)))
        out_shape.append(jax.ShapeDtypeStruct((m, D_GMLP), F32))
    return pl.pallas_call(
        functools.partial(_inproj_kernel, emit_vn=emit_vn),
        grid=(nt,),
        in_specs=[pl.BlockSpec((tm, D_MODEL), lambda i: (i, 0)),
                  vec(D_MODEL),
                  pl.BlockSpec((D_MODEL, D_IN), lambda i: (0, 0),
                               pipeline_mode=pl.Buffered(1)),
                  vec(HEAD_DIM), vec(HEAD_DIM), vec(GROUP_DIM), vec(GROUP_DIM)],
        out_specs=out_specs,
        out_shape=out_shape,
        compiler_params=pltpu.CompilerParams(
            dimension_semantics=("arbitrary",), vmem_limit_bytes=VMEM_LIMIT_BYTES),
        name="inproj",
    )(x2d, nmg, w_in, qg, kg, lng, lnb)


def _softmax_numer(sc):
    m = jnp.max(sc, axis=-1, keepdims=True)
    return jnp.exp2(sc - m).astype(BF16)


def _pv(p, v):
    ol = _dot(p, v)
    return ol[:, :HEAD_DIM] * (1.0 / ol[:, HEAD_DIM:])


def _qk(q, k):
    return lax.dot_general(q, k, (((1,), (1,)), ((), ())), preferred_element_type=F32)


def _attn_prompt_kernel(q_ref, k_ref, v_ref, bm_ref, o_ref, kp_ref, vp_ref):
    seq = q_ref.shape[0]
    zeros = jnp.zeros((ATTN_WINDOW, HEAD_DIM), BF16)
    kp_ref[0:ATTN_WINDOW, :] = zeros
    vp_ref[0:ATTN_WINDOW, 0:HEAD_DIM] = zeros
    kp_ref[ATTN_WINDOW:, :] = k_ref[...]
    vp_ref[ATTN_WINDOW:, 0:HEAD_DIM] = v_ref[...]
    vp_ref[:, HEAD_DIM:] = jnp.ones((seq + ATTN_WINDOW, HEAD_DIM), BF16)

    def group(q0, pre_stream):
        q = q_ref[pl.ds(q0, GQ), :]
        kb = kp_ref[pl.ds(q0, GK), :]
        vb = vp_ref[pl.ds(q0, GK), :]
        s = _qk(q, kb)
        rows = []
        for c in range(ATTN_GROUP):
            off = (c * CHUNK) // LANES * LANES
            variant = (c * CHUNK - off) // CHUNK
            sc = s[c * CHUNK:(c + 1) * CHUNK, off:off + BAND_PAD] + bm_ref[0, variant]
            if pre_stream:
                col = lax.broadcasted_iota(jnp.int32, (CHUNK, BAND_PAD), 1) + off
                sc = jnp.where(col >= ATTN_WINDOW - q0, sc, NEG_INF)
            parts = [_softmax_numer(sc)]
            if off:
                parts.insert(0, jnp.zeros((CHUNK, off), BF16))
            if GK - off - BAND_PAD:
                parts.append(jnp.zeros((CHUNK, GK - off - BAND_PAD), BF16))
            rows.append(jnp.concatenate(parts, axis=1))
        p = jnp.concatenate(rows, axis=0)
        o_ref[pl.ds(q0, GQ), :] = _pv(p, vb).astype(BF16)

    n_pre = ATTN_WINDOW // GQ
    for gi in range(n_pre):
        group(gi * GQ, True)

    def body(gi, carry):
        group(pl.multiple_of(gi * GQ, GQ), False)
        return carry

    lax.fori_loop(n_pre, seq // GQ, body, 0, unroll=True)


def _attn_prompt(z, bm, *, batch, seq):
    return pl.pallas_call(
        _attn_prompt_kernel,
        grid=(batch, N_HEADS),
        in_specs=[pl.BlockSpec((seq, HEAD_DIM), lambda b, h: (b, h)),
                  pl.BlockSpec((seq, HEAD_DIM), lambda b, h: (b, N_HEADS + h)),
                  pl.BlockSpec((seq, HEAD_DIM), lambda b, h: (b, 2 * N_HEADS + h)),
                  pl.BlockSpec((1, 2, CHUNK, BAND_PAD), lambda b, h: (h, 0, 0, 0))],
        out_specs=pl.BlockSpec((seq, HEAD_DIM), lambda b, h: (b, h)),
        out_shape=jax.ShapeDtypeStruct((batch * seq, D_ATTN), BF16),
        scratch_shapes=[pltpu.VMEM((seq + ATTN_WINDOW, HEAD_DIM), BF16),
                        pltpu.VMEM((seq + ATTN_WINDOW, 2 * HEAD_DIM), BF16)],
        compiler_params=pltpu.CompilerParams(
            dimension_semantics=("arbitrary", "arbitrary"),
            vmem_limit_bytes=VMEM_LIMIT_BYTES),
        name="attn_prompt",
    )(z, z, z, bm)


def _attn_sample_kernel(q_ref, kn_ref, vn_ref, kc_ref, vc_ref, bias_ref, o_ref):
    w = kc_ref.shape[0] // N_HEADS
    ones = jnp.ones((w + kn_ref.shape[0], HEAD_DIM), BF16)
    for h in range(N_HEADS):
        hc = slice(h * HEAD_DIM, (h + 1) * HEAD_DIM)
        head_rows = pl.ds(h, w, stride=N_HEADS)
        q = q_ref[:, hc]
        s = jnp.concatenate([_qk(q, kc_ref[head_rows, :].astype(BF16)),
                             _qk(q, kn_ref[:, hc])], axis=1)
        v = jnp.concatenate([vc_ref[head_rows, :].astype(BF16), vn_ref[:, hc]], axis=0)
        o = _pv(_softmax_numer(s + bias_ref[h]), jnp.concatenate([v, ones], axis=1))
        o_ref[:, hc] = o.astype(BF16)


def _attn_sample(z, k_cache, v_cache, bias, *, batch, t):
    w = k_cache.shape[0] // (batch * N_HEADS)
    return pl.pallas_call(
        _attn_sample_kernel,
        grid=(batch,),
        in_specs=[pl.BlockSpec((t, D_ATTN), lambda b: (b, 0)),
                  pl.BlockSpec((t, D_ATTN), lambda b: (b, 1)),
                  pl.BlockSpec((t, D_ATTN), lambda b: (b, 2)),
                  pl.BlockSpec((w * N_HEADS, HEAD_DIM), lambda b: (b, 0)),
                  pl.BlockSpec((w * N_HEADS, HEAD_DIM), lambda b: (b, 0)),
                  pl.BlockSpec((N_HEADS, t, w + t), lambda b: (0, 0, 0))],
        out_specs=pl.BlockSpec((t, D_ATTN), lambda b: (b, 0)),
        out_shape=jax.ShapeDtypeStruct((batch * t, D_ATTN), BF16),
        compiler_params=pltpu.CompilerParams(
            dimension_semantics=("arbitrary",), vmem_limit_bytes=VMEM_LIMIT_BYTES),
        name="attn_sample",
    )(z, z, z, k_cache, v_cache, bias)


def _outproj_kernel(oa_ref, u_ref, vn_ref, x_ref, wout_ref, ws_ref, bsb_ref, g_ref,
                    h_ref, n2_ref, *, chunk):
    tm = x_ref.shape[0]
    row = lax.broadcasted_iota(jnp.int32, (chunk, chunk), 0)
    col = lax.broadcasted_iota(jnp.int32, (chunk, chunk), 1)
    tri = row >= col
    wsg = [jnp.where(tri, ws_ref[g], 0.0).astype(BF16) for g in range(N_GROUPS)]
    nck = OUT_ROWS // chunk
    for r0 in range(0, tm, OUT_ROWS):
        ob = [[None] * N_GROUPS for _ in range(nck)]
        for g in range(N_GROUPS):
            gc = slice(g * GROUP_DIM, (g + 1) * GROUP_DIM)
            vn = jnp.concatenate([vn_ref[r0 + c * chunk:r0 + (c + 1) * chunk, gc]
                                  for c in range(nck)], axis=1)
            vs = _dot(wsg[g], vn)
            for c in range(nck):
                rows = slice(r0 + c * chunk, r0 + (c + 1) * chunk)
                gate = vs[:, c * GROUP_DIM:(c + 1) * GROUP_DIM] + bsb_ref[g]
                ob[c][g] = (u_ref[rows, gc].astype(F32) * gate).astype(BF16)
        ob = jnp.concatenate([jnp.concatenate(obc, axis=1) for obc in ob], axis=0)
        rows = slice(r0, r0 + OUT_ROWS)
        h = (x_ref[rows, :] + _dot(oa_ref[rows, :], wout_ref[0:D_ATTN, :])
             + _dot(ob, wout_ref[D_ATTN:, :]))
        h_ref[rows, :] = h
        ms = jnp.mean(h * h, axis=-1, keepdims=True)
        n2_ref[rows, :] = (h * lax.rsqrt(ms + EPS) * g_ref[...]).astype(BF16)


def _outproj(oa, z, x2d, w_out, ws, bsb, nfg, *, chunk):
    m = x2d.shape[0]
    tm = 512
    return pl.pallas_call(
        functools.partial(_outproj_kernel, chunk=chunk),
        grid=(m // tm,),
        in_specs=[pl.BlockSpec((tm, D_ATTN), lambda i: (i, 0)),
                  pl.BlockSpec((tm, D_GMLP), lambda i: (i, 3)),
                  pl.BlockSpec((tm, D_GMLP), lambda i: (i, 4)),
                  pl.BlockSpec((tm, D_MODEL), lambda i: (i, 0)),
                  pl.BlockSpec((D_MODEL, D_MODEL), lambda i: (0, 0),
                               pipeline_mode=pl.Buffered(1)),
                  pl.BlockSpec((N_GROUPS, chunk, chunk), lambda i: (0, 0, 0)),
                  pl.BlockSpec((N_GROUPS, chunk, GROUP_DIM), lambda i: (0, 0, 0)),
                  pl.BlockSpec((1, D_MODEL), lambda i: (0, 0))],
        out_specs=[pl.BlockSpec((tm, D_MODEL), lambda i: (i, 0)),
                   pl.BlockSpec((tm, D_MODEL), lambda i: (i, 0))],
        out_shape=[jax.ShapeDtypeStruct((m, D_MODEL), F32),
                   jax.ShapeDtypeStruct((m, D_MODEL), BF16)],
        compiler_params=pltpu.CompilerParams(
            dimension_semantics=("arbitrary",), vmem_limit_bytes=VMEM_LIMIT_BYTES),
        name="outproj",
    )(oa, z, z, x2d, w_out, ws, bsb, nfg)


def _silu(x):
    return x * (1.0 / (1.0 + jnp.exp(-x)))


def _up_kernel(n2_ref, w_ref, cw_ref, cb_ref, st_ref, m_ref, cs_ref,
               carry_ref, *, nseg, tiles_per_seq):
    i = pl.program_id(1)
    tm = n2_ref.shape[0]
    tn = m_ref.shape[1]
    sl = tm // nseg
    hist = SUBLANES
    sub = UP_SUB
    n2 = n2_ref[...]
    for sb in range(tn // sub):
        cs = slice(sb * sub, (sb + 1) * sub)
        w_sub = w_ref[:, 2 * sb * sub:2 * (sb + 1) * sub]
        ag = jnp.concatenate([_dot(n2[r:r + UP_DOT_ROWS, :], w_sub)
                              for r in range(0, tm, UP_DOT_ROWS)], axis=0)
        a = ag[:, :sub]
        gate = ag[:, sub:]
        w0 = cw_ref[0:1, cs]
        w1 = cw_ref[1:2, cs]
        w2 = cw_ref[2:3, cs]
        cb = cb_ref[:, cs]
        for r0 in range(0, tm, UP_ROWS):
            if r0 % sl:
                prev = a[r0 - hist:r0, :]
            elif nseg == 1:
                first = (i % tiles_per_seq) == 0
                prev = jnp.where(first, 0.0, carry_ref[:, cs])
            else:
                prev = st_ref[r0 // sl, :, cs]
            ac_rows = a[r0:r0 + UP_ROWS, :]
            win = jnp.concatenate([prev, ac_rows], axis=0)
            a_m1 = pltpu.roll(win, 1, 0)[hist:, :]
            a_m2 = pltpu.roll(win, 2, 0)[hist:, :]
            act = _silu(cb + w0 * a_m2 + w1 * a_m1 + w2 * ac_rows)
            m_ref[r0:r0 + UP_ROWS, cs] = (act * gate[r0:r0 + UP_ROWS, :]).astype(BF16)
        for s in range(nseg):
            tail = a[(s + 1) * sl - hist:(s + 1) * sl, :]
            if nseg == 1:
                carry_ref[:, cs] = tail
                cs_ref[0, :, cs] = tail
            else:
                cs_ref[s, :, cs] = tail


def _up(n2, wag, cw, cb, state8, *, nseg, tiles_per_seq):
    m = n2.shape[0]
    tm = 1024
    nt = m // tm
    tn = FF_TILE
    nj = D_FF_PAD // tn
    if nseg == 1:
        cs_rows = nt // tiles_per_seq
        cs_spec = pl.BlockSpec((1, SUBLANES, tn), lambda j, i: (i // tiles_per_seq, 0, j))
        st_spec = pl.BlockSpec((1, SUBLANES, tn), lambda j, i: (0, 0, j))
    else:
        cs_rows = nt * nseg
        cs_spec = pl.BlockSpec((nseg, SUBLANES, tn), lambda j, i: (i, 0, j))
        st_spec = pl.BlockSpec((nseg, SUBLANES, tn), lambda j, i: (i, 0, j))
    w_spec = pl.BlockSpec((D_MODEL, 2 * tn), lambda j, i: (0, j),
                          pipeline_mode=pl.Buffered(1))
    return pl.pallas_call(
        functools.partial(_up_kernel, nseg=nseg, tiles_per_seq=tiles_per_seq),
        grid=(nj, nt),
        in_specs=[pl.BlockSpec((tm, D_MODEL), lambda j, i: (i, 0)),
                  w_spec,
                  pl.BlockSpec((CONV_W, tn), lambda j, i: (0, j)),
                  pl.BlockSpec((1, tn), lambda j, i: (0, j)),
                  st_spec],
        out_specs=[pl.BlockSpec((tm, tn), lambda j, i: (i, j)), cs_spec],
        out_shape=[jax.ShapeDtypeStruct((m, D_FF_PAD), BF16),
                   jax.ShapeDtypeStruct((cs_rows, SUBLANES, D_FF_PAD), F32)],
        scratch_shapes=[pltpu.VMEM((SUBLANES, tn), F32)],
        compiler_params=pltpu.CompilerParams(
            dimension_semantics=("arbitrary", "arbitrary"),
            vmem_limit_bytes=VMEM_LIMIT_BYTES),
        name="up",
    )(n2, wag, cw, cb, state8)


def _down_kernel(m_ref, w_ref, h_ref, y_ref):
    y_ref[...] = h_ref[...] + _dot(m_ref[...], w_ref[...])


def _down(mm, w_down, h):
    m = mm.shape[0]
    tm = 512
    return pl.pallas_call(
        _down_kernel,
        grid=(m // tm,),
        in_specs=[pl.BlockSpec((tm, D_FF_PAD), lambda i: (i, 0)),
                  pl.BlockSpec((D_FF_PAD, D_MODEL), lambda i: (0, 0),
                               pipeline_mode=pl.Buffered(1)),
                  pl.BlockSpec((tm, D_MODEL), lambda i: (i, 0))],
        out_specs=pl.BlockSpec((tm, D_MODEL), lambda i: (i, 0)),
        out_shape=jax.ShapeDtypeStruct((m, D_MODEL), F32),
        compiler_params=pltpu.CompilerParams(
            dimension_semantics=("arbitrary",), vmem_limit_bytes=VMEM_LIMIT_BYTES),
        name="down",
    )(mm, w_down, h)


def _band_bias(table):
    d = jnp.arange(-(CHUNK - 1), BAND)
    e = table[:, jnp.clip(ATTN_WINDOW - d, -REL_CLIP, REL_CLIP) + REL_CLIP].astype(F32)
    n = e.shape[1]
    ep = jnp.pad(e, ((0, 0), (0, 1)))
    toep = jnp.tile(ep, (1, CHUNK))[:, :CHUNK * n].reshape(N_HEADS, CHUNK, n)
    return toep[:, :, CHUNK - 1:CHUNK - 1 + BAND] * LOG2E


def _group_bias(bias):
    pad = BAND_PAD - BAND
    even = jnp.pad(bias, ((0, 0), (0, 0), (0, pad)), constant_values=NEG_INF)
    odd = jnp.pad(bias, ((0, 0), (0, 0), (pad, 0)), constant_values=NEG_INF)
    return jnp.stack([even, odd], axis=1)


def _cast_up_kernel(w_ref, o_ref):
    sub = UP_SUB
    for c in range(D_FF_PAD // sub):
        for half in range(2):
            lo = c * sub
            n = min(sub, D_FF - lo)
            dst = (2 * c + half) * sub
            o_ref[:, dst:dst + n] = w_ref[:, half * D_FF + lo:half * D_FF + lo + n].astype(BF16)
            if n < sub:
                o_ref[:, dst + n:dst + sub] = jnp.zeros((w_ref.shape[0], sub - n), BF16)


def _cast_up_weight(w_up):
    tr = 128
    return pl.pallas_call(
        _cast_up_kernel,
        grid=(D_MODEL // tr,),
        in_specs=[pl.BlockSpec((tr, 2 * D_FF), lambda i: (i, 0))],
        out_specs=pl.BlockSpec((tr, 2 * D_FF_PAD), lambda i: (i, 0)),
        out_shape=jax.ShapeDtypeStruct((D_MODEL, 2 * D_FF_PAD), BF16),
        compiler_params=pltpu.CompilerParams(
            dimension_semantics=("arbitrary",), vmem_limit_bytes=VMEM_LIMIT_BYTES),
        name="cast_up",
    )(w_up)


def _cast_down_kernel(w_ref, o_ref):
    tr = w_ref.shape[0]
    row = lax.broadcasted_iota(jnp.int32, w_ref.shape, 0) + pl.program_id(0) * tr
    o_ref[...] = jnp.where(row < D_FF, w_ref[...], 0.0).astype(BF16)


def _cast_down_weight(w_down):
    tr = 256
    return pl.pallas_call(
        _cast_down_kernel,
        grid=(D_FF_PAD // tr,),
        in_specs=[pl.BlockSpec((tr, D_MODEL), lambda i: (i, 0))],
        out_specs=pl.BlockSpec((tr, D_MODEL), lambda i: (i, 0)),
        out_shape=jax.ShapeDtypeStruct((D_FF_PAD, D_MODEL), BF16),
        compiler_params=pltpu.CompilerParams(
            dimension_semantics=("arbitrary",), vmem_limit_bytes=VMEM_LIMIT_BYTES),
        name="cast_down",
    )(w_down)


def _prep_weights(w_in, w_out, w_up, cw, cb, w_down):
    pad = D_FF_PAD - D_FF
    wag = _cast_up_weight(w_up)
    wd = _cast_down_weight(w_down)
    cwp = jnp.pad(cw, ((0, 0), (0, pad)))
    cbp = jnp.pad(cb, ((0, pad),))[None, :]
    return w_in.astype(BF16), w_out.astype(BF16), wag, cwp, cbp, wd


def _layer(x2d, seq, sample_cache, nmg, w_in, qg, kg, bias, lng, lnb, ws, bs, w_out, nfg,
           wag, cw, cb, wd):
    m = x2d.shape[0]
    batch = m // seq
    is_sample = sample_cache is not None
    row = lambda v: v[None, :]
    if is_sample:
        z, kf, vf, vnf = _inproj(x2d, row(nmg), w_in, row(qg), row(kg), row(lng), row(lnb),
                                 keep_every=1, emit_vn=True)
        ck, cv, cst = sample_cache
        w_cache = ck.shape[1]
        oa = _attn_sample(z, ck.reshape(batch * w_cache * N_HEADS, HEAD_DIM),
                          cv.reshape(batch * w_cache * N_HEADS, HEAD_DIM),
                          bias[:, :, BAND - w_cache - seq:], batch=batch, t=seq)
        chunk = seq
        state8 = jnp.pad(cst, ((0, 0), (SUBLANES - (CONV_W - 1), 0), (0, D_FF_PAD - D_FF)))
        nseg, tiles_per_seq = batch, 1
    else:
        keep_every = seq // 512
        z, kf, vf = _inproj(x2d, row(nmg), w_in, row(qg), row(kg), row(lng), row(lnb),
                            keep_every=keep_every, emit_vn=False)
        vnf = None
        oa = _attn_prompt(z, _group_bias(bias), batch=batch, seq=seq)
        chunk = GMLP_CHUNK
        state8 = jnp.zeros((1, SUBLANES, D_FF_PAD), F32)
        nseg, tiles_per_seq = 1, seq // 1024
    wsl = ws[:, :chunk, :chunk]
    bsb = jnp.broadcast_to(bs[:, :chunk, None], (N_GROUPS, chunk, GROUP_DIM))
    h, n2 = _outproj(oa, z, x2d, w_out, wsl, bsb, row(nfg), chunk=chunk)
    mm, cs = _up(n2, wag, cw, cb, state8, nseg=nseg, tiles_per_seq=tiles_per_seq)
    y = _down(mm, wd, h)
    conv_state = cs[:, SUBLANES - (CONV_W - 1):, :D_FF]
    return y, kf, vf, vnf, conv_state


def kernel(x_prompt, x_sample, cache_attn_k, cache_attn_v, state_ffn_conv, norm_mix_g, w_in,
           q_norm_g, k_norm_g, rel_bias_table, gmlp_ln_g, gmlp_ln_b, gmlp_w_s, gmlp_b_s, w_out,
           norm_ffn_g, w_up, ffn_conv_w, ffn_conv_b, w_down):
    batch, seq, _ = x_prompt.shape
    dbatch, dseq, _ = x_sample.shape
    depth = w_in.shape[0]
    xp = x_prompt.reshape(batch * seq, D_MODEL)
    xs = x_sample.reshape(dbatch * dseq, D_MODEL)
    keep = min(ATTN_WINDOW, seq)
    outs = [[] for _ in range(7)]
    for l in range(depth):
        w_in_b, w_out_b, wag, cw, cb, wd = _prep_weights(
            w_in[l], w_out[l], w_up[l], ffn_conv_w[l], ffn_conv_b[l], w_down[l])
        bias = _band_bias(rel_bias_table[l])
        shared = (norm_mix_g[l], w_in_b, q_norm_g[l], k_norm_g[l], bias, gmlp_ln_g[l],
                  gmlp_ln_b[l], gmlp_w_s[l], gmlp_b_s[l], w_out_b, norm_ffn_g[l],
                  wag, cw, cb, wd)
        xp, kp, vp, _, cp = _layer(xp, seq, None, *shared)
        xs, ks, vs, gs, cs = _layer(
            xs, dseq, (cache_attn_k[l], cache_attn_v[l], state_ffn_conv[l]), *shared)
        outs[0].append(kp.reshape(batch, keep, N_HEADS, HEAD_DIM))
        outs[1].append(vp.reshape(batch, keep, N_HEADS, HEAD_DIM))
        outs[2].append(cp)
        outs[3].append(ks.reshape(dbatch, dseq, N_HEADS, HEAD_DIM))
        outs[4].append(vs.reshape(dbatch, dseq, N_HEADS, HEAD_DIM))
        outs[5].append(gs.reshape(dbatch, dseq, D_GMLP))
        outs[6].append(cs)
    return (xp.reshape(batch, seq, D_MODEL), xs.reshape(dbatch, dseq, D_MODEL),
            *[jnp.stack(o) for o in outs])
```

```python
import functools

import jax
import jax.numpy as jnp
from jax import lax
from jax.experimental import pallas as pl
from jax.experimental.pallas import tpu as pltpu

D_MODEL = 2048
CHUNK = 64
ATTN_WINDOW = 8 * CHUNK
BAND = ATTN_WINDOW + CHUNK
D_ATTN = D_MODEL // 2
HEAD_DIM = 128
N_HEADS = D_ATTN // HEAD_DIM
D_GMLP = D_MODEL - D_ATTN
N_GROUPS = 8
GROUP_DIM = D_GMLP // N_GROUPS
GMLP_CHUNK = 128
REL_CLIP = 128
D_FF = 5504
CONV_W = 3
D_IN = 3 * D_ATTN + 2 * D_GMLP
EPS = 1e-6
NEG_INF = -1e30
LOG2E = 1.4426950408889634

LANES = 128
SUBLANES = 8
BF16_ROWS = 16
VMEM_LIMIT_BYTES = 60 * 1024 * 1024

FF_TILE = 2816
D_FF_PAD = 2 * FF_TILE
UP_SUB = 256
UP_ROWS = 64
UP_DOT_ROWS = 256
OUT_ROWS = 256
IN_DOT_ROWS = 256
ATTN_GROUP = 4
GQ = ATTN_GROUP * CHUNK
GK = GQ + ATTN_WINDOW
BAND_PAD = BAND + CHUNK
assert BAND_PAD % LANES == 0 and LANES == 2 * CHUNK

F32 = jnp.float32
BF16 = jnp.bfloat16


def _dot(a, b):
    return jnp.dot(a, b, preferred_element_type=F32)


def _gelu(x):
    return 0.5 * x * (1.0 + lax.erf(x * (0.5 ** 0.5)))


def _inproj_kernel(x_ref, nmg_ref, w_ref, qg_ref, kg_ref, lng_ref, lnb_ref,
                   z_ref, kf_ref, vf_ref, *rest, emit_vn):
    x = x_ref[...]
    ms = jnp.mean(x * x, axis=-1, keepdims=True)
    n = (x * lax.rsqrt(ms + EPS) * nmg_ref[...]).astype(BF16)

    def head_rms(zh, g):
        r = lax.rsqrt(jnp.mean(zh * zh, axis=-1, keepdims=True) + EPS)
        return zh * r * g

    sub = 4 * HEAD_DIM
    order = sorted(range(D_IN // sub), key=lambda sb: (4, 3, 0, 1, 2).index(sb * sub // D_ATTN))
    for sb in order:
        seg = (sb * sub) // D_ATTN
        w_sub = w_ref[:, sb * sub:(sb + 1) * sub]
        zs = jnp.concatenate([_dot(n[r:r + IN_DOT_ROWS, :], w_sub)
                              for r in range(0, n.shape[0], IN_DOT_ROWS)], axis=0)
        for hh in range(sub // HEAD_DIM):
            c0 = sb * sub + hh * HEAD_DIM
            lo = c0 - seg * D_ATTN
            zh = zs[:, hh * HEAD_DIM:(hh + 1) * HEAD_DIM]
            if seg == 0:
                out = head_rms(zh, qg_ref[...] * (HEAD_DIM ** -0.5 * LOG2E))
            elif seg == 1:
                out = head_rms(zh, kg_ref[...])
            elif seg == 2:
                out = zh
            elif seg == 3:
                out = _gelu(zh)
            else:
                ge = _gelu(zh)
                mu = jnp.mean(ge, axis=-1, keepdims=True)
                d = ge - mu
                var = jnp.mean(d * d, axis=-1, keepdims=True)
                out = d * lax.rsqrt(var + EPS) * lng_ref[...] + lnb_ref[...]
            z_ref[:, c0:c0 + HEAD_DIM] = out.astype(BF16)
            head_rows = pl.ds(lo // HEAD_DIM, zh.shape[0], stride=N_HEADS)
            if seg == 1:
                kf_ref[head_rows, :] = out
            elif seg == 2:
                vf_ref[head_rows, :] = out
            elif seg == 4 and emit_vn:
                rest[0][:, lo:lo + HEAD_DIM] = out


def _inproj(x2d, nmg, w_in, qg, kg, lng, lnb, *, keep_every, emit_vn):
    m = x2d.shape[0]
    tm = 512
    nt = m // tm
    nkeep = nt // keep_every
    vec = lambda width: pl.BlockSpec((1, width), lambda i: (0, 0))
    keep_spec = pl.BlockSpec((tm * N_HEADS, HEAD_DIM), lambda i: (i // keep_every, 0))
    out_specs = [pl.BlockSpec((tm, D_IN), lambda i: (i, 0)), keep_spec, keep_spec]
    out_shape = [jax.ShapeDtypeStruct((m, D_IN), BF16),
                 jax.ShapeDtypeStruct((nkeep * tm * N_HEADS, HEAD_DIM), F32),
                 jax.ShapeDtypeStruct((nkeep * tm * N_HEADS, HEAD_DIM), F32)]
    if emit_vn:
        out_specs.append(pl.BlockSpec((tm, D_GMLP), lambda i: (i, 0)))
        out_shape.append(jax.ShapeDtypeStruct((m, D_GMLP), F32))
    return pl.pallas_call(
        functools.partial(_inproj_kernel, emit_vn=emit_vn),
        grid=(nt,),
        in_specs=[pl.BlockSpec((tm, D_MODEL), lambda i: (i, 0)),
                  vec(D_MODEL),
                  pl.BlockSpec((D_MODEL, D_IN), lambda i: (0, 0),
                               pipeline_mode=pl.Buffered(1)),
                  vec(HEAD_DIM), vec(HEAD_DIM), vec(GROUP_DIM), vec(GROUP_DIM)],
        out_specs=out_specs,
        out_shape=out_shape,
        compiler_params=pltpu.CompilerParams(
            dimension_semantics=("arbitrary",), vmem_limit_bytes=VMEM_LIMIT_BYTES),
        name="inproj",
    )(x2d, nmg, w_in, qg, kg, lng, lnb)


def _softmax_numer(sc):
    m = jnp.max(sc, axis=-1, keepdims=True)
    return jnp.exp2(sc - m).astype(BF16)


def _pv(p, v):
    ol = _dot(p, v)
    return ol[:, :HEAD_DIM] * (1.0 / ol[:, HEAD_DIM:])


def _qk(q, k):
    return lax.dot_general(q, k, (((1,), (1,)), ((), ())), preferred_element_type=F32)


def _attn_prompt_kernel(q_ref, k_ref, v_ref, bm_ref, wup_ref, wdn_ref,
                        o_ref, wag_ref, wd_ref, kp_ref, vp_ref):
    step = pl.program_id(0) * pl.num_programs(1) + pl.program_id(1)
    _cast_up_rows(wup_ref, wag_ref)
    _cast_down_rows(wdn_ref, wd_ref, step)
    seq = q_ref.shape[0]
    zeros = jnp.zeros((ATTN_WINDOW, HEAD_DIM), BF16)
    kp_ref[0:ATTN_WINDOW, :] = zeros
    vp_ref[0:ATTN_WINDOW, 0:HEAD_DIM] = zeros
    kp_ref[ATTN_WINDOW:, :] = k_ref[...]
    vp_ref[ATTN_WINDOW:, 0:HEAD_DIM] = v_ref[...]
    vp_ref[:, HEAD_DIM:] = jnp.ones((seq + ATTN_WINDOW, HEAD_DIM), BF16)

    def group(q0, pre_stream):
        q = q_ref[pl.ds(q0, GQ), :]
        kb = kp_ref[pl.ds(q0, GK), :]
        vb = vp_ref[pl.ds(q0, GK), :]
        s = _qk(q, kb)
        rows = []
        for c in range(ATTN_GROUP):
            off = (c * CHUNK) // LANES * LANES
            variant = (c * CHUNK - off) // CHUNK
            sc = s[c * CHUNK:(c + 1) * CHUNK, off:off + BAND_PAD] + bm_ref[0, variant]
            if pre_stream:
                col = lax.broadcasted_iota(jnp.int32, (CHUNK, BAND_PAD), 1) + off
                sc = jnp.where(col >= ATTN_WINDOW - q0, sc, NEG_INF)
            parts = [_softmax_numer(sc)]
            if off:
                parts.insert(0, jnp.zeros((CHUNK, off), BF16))
            if GK - off - BAND_PAD:
                parts.append(jnp.zeros((CHUNK, GK - off - BAND_PAD), BF16))
            rows.append(jnp.concatenate(parts, axis=1))
        p = jnp.concatenate(rows, axis=0)
        o_ref[pl.ds(q0, GQ), :] = _pv(p, vb).astype(BF16)

    n_pre = ATTN_WINDOW // GQ
    for gi in range(n_pre):
        group(gi * GQ, True)

    def body(gi, carry):
        group(pl.multiple_of(gi * GQ, GQ), False)
        return carry

    lax.fori_loop(n_pre, seq // GQ, body, 0, unroll=True)


def _attn_prompt(z, bm, w_up, w_down, *, batch, seq):
    steps = batch * N_HEADS
    up_rows = D_MODEL // steps
    down_rows = -(-D_FF_PAD // (steps * BF16_ROWS)) * BF16_ROWS
    assert up_rows * steps == D_MODEL and up_rows % BF16_ROWS == 0
    last_down = (D_FF - 1) // down_rows
    step = lambda b, h: b * N_HEADS + h
    return pl.pallas_call(
        _attn_prompt_kernel,
        grid=(batch, N_HEADS),
        in_specs=[pl.BlockSpec((seq, HEAD_DIM), lambda b, h: (b, h)),
                  pl.BlockSpec((seq, HEAD_DIM), lambda b, h: (b, N_HEADS + h)),
                  pl.BlockSpec((seq, HEAD_DIM), lambda b, h: (b, 2 * N_HEADS + h)),
                  pl.BlockSpec((1, 2, CHUNK, BAND_PAD), lambda b, h: (h, 0, 0, 0)),
                  pl.BlockSpec((up_rows, 2 * D_FF), lambda b, h: (step(b, h), 0)),
                  pl.BlockSpec((down_rows, D_MODEL),
                               lambda b, h: (jnp.minimum(step(b, h), last_down), 0))],
        out_specs=[pl.BlockSpec((seq, HEAD_DIM), lambda b, h: (b, h)),
                   pl.BlockSpec((up_rows, 2 * D_FF_PAD), lambda b, h: (step(b, h), 0)),
                   pl.BlockSpec((down_rows, D_MODEL), lambda b, h: (step(b, h), 0))],
        out_shape=[jax.ShapeDtypeStruct((batch * seq, D_ATTN), BF16),
                   jax.ShapeDtypeStruct((D_MODEL, 2 * D_FF_PAD), BF16),
                   jax.ShapeDtypeStruct((steps * down_rows, D_MODEL), BF16)],
        scratch_shapes=[pltpu.VMEM((seq + ATTN_WINDOW, HEAD_DIM), BF16),
                        pltpu.VMEM((seq + ATTN_WINDOW, 2 * HEAD_DIM), BF16)],
        compiler_params=pltpu.CompilerParams(
            dimension_semantics=("arbitrary", "arbitrary"),
            vmem_limit_bytes=VMEM_LIMIT_BYTES),
        name="attn_prompt",
    )(z, z, z, bm, w_up, w_down)


def _attn_sample_kernel(q_ref, kn_ref, vn_ref, kc_ref, vc_ref, bias_ref, o_ref):
    w = kc_ref.shape[0] // N_HEADS
    ones = jnp.ones((w + kn_ref.shape[0], HEAD_DIM), BF16)
    for h in range(N_HEADS):
        hc = slice(h * HEAD_DIM, (h + 1) * HEAD_DIM)
        head_rows = pl.ds(h, w, stride=N_HEADS)
        q = q_ref[:, hc]
        s = jnp.concatenate([_qk(q, kc_ref[head_rows, :].astype(BF16)),
                             _qk(q, kn_ref[:, hc])], axis=1)
        v = jnp.concatenate([vc_ref[head_rows, :].astype(BF16), vn_ref[:, hc]], axis=0)
        o = _pv(_softmax_numer(s + bias_ref[h]), jnp.concatenate([v, ones], axis=1))
        o_ref[:, hc] = o.astype(BF16)


def _attn_sample(z, k_cache, v_cache, bias, *, batch, t):
    w = k_cache.shape[0] // (batch * N_HEADS)
    return pl.pallas_call(
        _attn_sample_kernel,
        grid=(batch,),
        in_specs=[pl.BlockSpec((t, D_ATTN), lambda b: (b, 0)),
                  pl.BlockSpec((t, D_ATTN), lambda b: (b, 1)),
                  pl.BlockSpec((t, D_ATTN), lambda b: (b, 2)),
                  pl.BlockSpec((w * N_HEADS, HEAD_DIM), lambda b: (b, 0)),
                  pl.BlockSpec((w * N_HEADS, HEAD_DIM), lambda b: (b, 0)),
                  pl.BlockSpec((N_HEADS, t, w + t), lambda b: (0, 0, 0))],
        out_specs=pl.BlockSpec((t, D_ATTN), lambda b: (b, 0)),
        out_shape=jax.ShapeDtypeStruct((batch * t, D_ATTN), BF16),
        compiler_params=pltpu.CompilerParams(
            dimension_semantics=("arbitrary",), vmem_limit_bytes=VMEM_LIMIT_BYTES),
        name="attn_sample",
    )(z, z, z, k_cache, v_cache, bias)


def _outproj_kernel(oa_ref, u_ref, vn_ref, x_ref, wout_ref, ws_ref, bsb_ref, g_ref,
                    h_ref, n2_ref, *, chunk):
    tm = x_ref.shape[0]
    row = lax.broadcasted_iota(jnp.int32, (chunk, chunk), 0)
    col = lax.broadcasted_iota(jnp.int32, (chunk, chunk), 1)
    tri = row >= col
    wsg = [jnp.where(tri, ws_ref[g], 0.0).astype(BF16) for g in range(N_GROUPS)]
    nck = OUT_ROWS // chunk
    for r0 in range(0, tm, OUT_ROWS):
        ob = [[None] * N_GROUPS for _ in range(nck)]
        for g in range(N_GROUPS):
            gc = slice(g * GROUP_DIM, (g + 1) * GROUP_DIM)
            vn = jnp.concatenate([vn_ref[r0 + c * chunk:r0 + (c + 1) * chunk, gc]
                                  for c in range(nck)], axis=1)
            vs = _dot(wsg[g], vn)
            for c in range(nck):
                rows = slice(r0 + c * chunk, r0 + (c + 1) * chunk)
                gate = vs[:, c * GROUP_DIM:(c + 1) * GROUP_DIM] + bsb_ref[g]
                ob[c][g] = (u_ref[rows, gc].astype(F32) * gate).astype(BF16)
        ob = jnp.concatenate([jnp.concatenate(obc, axis=1) for obc in ob], axis=0)
        rows = slice(r0, r0 + OUT_ROWS)
        h = (x_ref[rows, :] + _dot(oa_ref[rows, :], wout_ref[0:D_ATTN, :])
             + _dot(ob, wout_ref[D_ATTN:, :]))
        h_ref[rows, :] = h
        ms = jnp.mean(h * h, axis=-1, keepdims=True)
        n2_ref[rows, :] = (h * lax.rsqrt(ms + EPS) * g_ref[...]).astype(BF16)


def _outproj(oa, z, x2d, w_out, ws, bsb, nfg, *, chunk):
    m = x2d.shape[0]
    tm = 512
    return pl.pallas_call(
        functools.partial(_outproj_kernel, chunk=chunk),
        grid=(m // tm,),
        in_specs=[pl.BlockSpec((tm, D_ATTN), lambda i: (i, 0)),
                  pl.BlockSpec((tm, D_GMLP), lambda i: (i, 3)),
                  pl.BlockSpec((tm, D_GMLP), lambda i: (i, 4)),
                  pl.BlockSpec((tm, D_MODEL), lambda i: (i, 0)),
                  pl.BlockSpec((D_MODEL, D_MODEL), lambda i: (0, 0),
                               pipeline_mode=pl.Buffered(1)),
                  pl.BlockSpec((N_GROUPS, chunk, chunk), lambda i: (0, 0, 0)),
                  pl.BlockSpec((N_GROUPS, chunk, GROUP_DIM), lambda i: (0, 0, 0)),
                  pl.BlockSpec((1, D_MODEL), lambda i: (0, 0))],
        out_specs=[pl.BlockSpec((tm, D_MODEL), lambda i: (i, 0)),
                   pl.BlockSpec((tm, D_MODEL), lambda i: (i, 0))],
        out_shape=[jax.ShapeDtypeStruct((m, D_MODEL), F32),
                   jax.ShapeDtypeStruct((m, D_MODEL), BF16)],
        compiler_params=pltpu.CompilerParams(
            dimension_semantics=("arbitrary",), vmem_limit_bytes=VMEM_LIMIT_BYTES),
        name="outproj",
    )(oa, z, z, x2d, w_out, ws, bsb, nfg)


def _silu(x):
    return x * (1.0 / (1.0 + jnp.exp(-x)))


def _up_kernel(n2_ref, w_ref, cw_ref, cb_ref, st_ref, m_ref, cs_ref,
               carry_ref, *, nseg, tiles_per_seq):
    i = pl.program_id(1)
    tm = n2_ref.shape[0]
    tn = m_ref.shape[1]
    sl = tm // nseg
    hist = SUBLANES
    sub = UP_SUB
    n2 = n2_ref[...]
    for sb in range(tn // sub):
        cs = slice(sb * sub, (sb + 1) * sub)
        w_sub = w_ref[:, 2 * sb * sub:2 * (sb + 1) * sub]
        ag = jnp.concatenate([_dot(n2[r:r + UP_DOT_ROWS, :], w_sub)
                              for r in range(0, tm, UP_DOT_ROWS)], axis=0)
        a = ag[:, :sub]
        gate = ag[:, sub:]
        w0 = cw_ref[0:1, cs]
        w1 = cw_ref[1:2, cs]
        w2 = cw_ref[2:3, cs]
        cb = cb_ref[:, cs]
        for r0 in range(0, tm, UP_ROWS):
            if r0 % sl:
                prev = a[r0 - hist:r0, :]
            elif nseg == 1:
                first = (i % tiles_per_seq) == 0
                prev = jnp.where(first, 0.0, carry_ref[:, cs])
            else:
                prev = st_ref[r0 // sl, :, cs]
            ac_rows = a[r0:r0 + UP_ROWS, :]
            win = jnp.concatenate([prev, ac_rows], axis=0)
            a_m1 = pltpu.roll(win, 1, 0)[hist:, :]
            a_m2 = pltpu.roll(win, 2, 0)[hist:, :]
            act = _silu(cb + w0 * a_m2 + w1 * a_m1 + w2 * ac_rows)
            m_ref[r0:r0 + UP_ROWS, cs] = (act * gate[r0:r0 + UP_ROWS, :]).astype(BF16)
        for s in range(nseg):
            tail = a[(s + 1) * sl - hist:(s + 1) * sl, :]
            if nseg == 1:
                carry_ref[:, cs] = tail
                cs_ref[0, :, cs] = tail
            else:
                cs_ref[s, :, cs] = tail


def _up(n2, wag, cw, cb, state8, *, nseg, tiles_per_seq):
    m = n2.shape[0]
    tm = 1024
    nt = m // tm
    tn = FF_TILE
    nj = D_FF_PAD // tn
    if nseg == 1:
        cs_rows = nt // tiles_per_seq
        cs_spec = pl.BlockSpec((1, SUBLANES, tn), lambda j, i: (i // tiles_per_seq, 0, j))
        st_spec = pl.BlockSpec((1, SUBLANES, tn), lambda j, i: (0, 0, j))
    else:
        cs_rows = nt * nseg
        cs_spec = pl.BlockSpec((nseg, SUBLANES, tn), lambda j, i: (i, 0, j))
        st_spec = pl.BlockSpec((nseg, SUBLANES, tn), lambda j, i: (i, 0, j))
    w_spec = pl.BlockSpec((D_MODEL, 2 * tn), lambda j, i: (0, j),
                          pipeline_mode=pl.Buffered(1))
    return pl.pallas_call(
        functools.partial(_up_kernel, nseg=nseg, tiles_per_seq=tiles_per_seq),
        grid=(nj, nt),
        in_specs=[pl.BlockSpec((tm, D_MODEL), lambda j, i: (i, 0)),
                  w_spec,
                  pl.BlockSpec((CONV_W, tn), lambda j, i: (0, j)),
                  pl.BlockSpec((1, tn), lambda j, i: (0, j)),
                  st_spec],
        out_specs=[pl.BlockSpec((tm, tn), lambda j, i: (i, j)), cs_spec],
        out_shape=[jax.ShapeDtypeStruct((m, D_FF_PAD), BF16),
                   jax.ShapeDtypeStruct((cs_rows, SUBLANES, D_FF_PAD), F32)],
        scratch_shapes=[pltpu.VMEM((SUBLANES, tn), F32)],
        compiler_params=pltpu.CompilerParams(
            dimension_semantics=("arbitrary", "arbitrary"),
            vmem_limit_bytes=VMEM_LIMIT_BYTES),
        name="up",
    )(n2, wag, cw, cb, state8)


def _down_kernel(m_ref, w_ref, h_ref, y_ref):
    y_ref[...] = h_ref[...] + _dot(m_ref[...], w_ref[...])


def _down(mm, w_down, h):
    m = mm.shape[0]
    tm = 512
    return pl.pallas_call(
        _down_kernel,
        grid=(m // tm,),
        in_specs=[pl.BlockSpec((tm, D_FF_PAD), lambda i: (i, 0)),
                  pl.BlockSpec((D_FF_PAD, D_MODEL), lambda i: (0, 0),
                               pipeline_mode=pl.Buffered(1)),
                  pl.BlockSpec((tm, D_MODEL), lambda i: (i, 0))],
        out_specs=pl.BlockSpec((tm, D_MODEL), lambda i: (i, 0)),
        out_shape=jax.ShapeDtypeStruct((m, D_MODEL), F32),
        compiler_params=pltpu.CompilerParams(
            dimension_semantics=("arbitrary",), vmem_limit_bytes=VMEM_LIMIT_BYTES),
        name="down",
    )(mm, w_down, h)


def _band_bias(table):
    d = jnp.arange(-(CHUNK - 1), BAND)
    e = table[:, jnp.clip(ATTN_WINDOW - d, -REL_CLIP, REL_CLIP) + REL_CLIP].astype(F32)
    n = e.shape[1]
    ep = jnp.pad(e, ((0, 0), (0, 1)))
    toep = jnp.tile(ep, (1, CHUNK))[:, :CHUNK * n].reshape(N_HEADS, CHUNK, n)
    return toep[:, :, CHUNK - 1:CHUNK - 1 + BAND] * LOG2E


def _group_bias(bias):
    pad = BAND_PAD - BAND
    even = jnp.pad(bias, ((0, 0), (0, 0), (0, pad)), constant_values=NEG_INF)
    odd = jnp.pad(bias, ((0, 0), (0, 0), (pad, 0)), constant_values=NEG_INF)
    return jnp.stack([even, odd], axis=1)


def _cast_up_rows(w_ref, o_ref):
    sub = UP_SUB
    for c in range(D_FF_PAD // sub):
        for half in range(2):
            lo = c * sub
            n = min(sub, D_FF - lo)
            dst = (2 * c + half) * sub
            o_ref[:, dst:dst + n] = w_ref[:, half * D_FF + lo:half * D_FF + lo + n].astype(BF16)
            if n < sub:
                o_ref[:, dst + n:dst + sub] = jnp.zeros((w_ref.shape[0], sub - n), BF16)


def _cast_down_rows(w_ref, o_ref, block):
    tr = w_ref.shape[0]
    row = lax.broadcasted_iota(jnp.int32, w_ref.shape, 0) + block * tr
    o_ref[...] = jnp.where(row < D_FF, w_ref[...], 0.0).astype(BF16)


def _layer(x2d, seq, sample_cache, ffn_weights, nmg, w_in, qg, kg, bias, lng, lnb, ws, bs,
           w_out, nfg, cw, cb):
    m = x2d.shape[0]
    batch = m // seq
    is_sample = sample_cache is not None
    row = lambda v: v[None, :]
    if is_sample:
        wag, wd = ffn_weights
        z, kf, vf, vnf = _inproj(x2d, row(nmg), w_in, row(qg), row(kg), row(lng), row(lnb),
                                 keep_every=1, emit_vn=True)
        ck, cv, cst = sample_cache
        w_cache = ck.shape[1]
        oa = _attn_sample(z, ck.reshape(batch * w_cache * N_HEADS, HEAD_DIM),
                          cv.reshape(batch * w_cache * N_HEADS, HEAD_DIM),
                          bias[:, :, BAND - w_cache - seq:], batch=batch, t=seq)
        chunk = seq
        state8 = jnp.pad(cst, ((0, 0), (SUBLANES - (CONV_W - 1), 0), (0, D_FF_PAD - D_FF)))
        nseg, tiles_per_seq = batch, 1
    else:
        w_up, w_down = ffn_weights
        keep_every = seq // 512
        z, kf, vf = _inproj(x2d, row(nmg), w_in, row(qg), row(kg), row(lng), row(lnb),
                            keep_every=keep_every, emit_vn=False)
        vnf = None
        oa, wag, wd = _attn_prompt(z, _group_bias(bias), w_up, w_down, batch=batch, seq=seq)
        chunk = GMLP_CHUNK
        state8 = jnp.zeros((1, SUBLANES, D_FF_PAD), F32)
        nseg, tiles_per_seq = 1, seq // 1024
    wsl = ws[:, :chunk, :chunk]
    bsb = jnp.broadcast_to(bs[:, :chunk, None], (N_GROUPS, chunk, GROUP_DIM))
    h, n2 = _outproj(oa, z, x2d, w_out, wsl, bsb, row(nfg), chunk=chunk)
    mm, cs = _up(n2, wag, cw, cb, state8, nseg=nseg, tiles_per_seq=tiles_per_seq)
    y = _down(mm, wd, h)
    conv_state = cs[:, SUBLANES - (CONV_W - 1):, :D_FF]
    return y, kf, vf, vnf, conv_state, (wag, wd)


def kernel(x_prompt, x_sample, cache_attn_k, cache_attn_v, state_ffn_conv, norm_mix_g, w_in,
           q_norm_g, k_norm_g, rel_bias_table, gmlp_ln_g, gmlp_ln_b, gmlp_w_s, gmlp_b_s, w_out,
           norm_ffn_g, w_up, ffn_conv_w, ffn_conv_b, w_down):
    batch, seq, _ = x_prompt.shape
    dbatch, dseq, _ = x_sample.shape
    depth = w_in.shape[0]
    xp = x_prompt.reshape(batch * seq, D_MODEL)
    xs = x_sample.reshape(dbatch * dseq, D_MODEL)
    keep = min(ATTN_WINDOW, seq)
    pad = D_FF_PAD - D_FF
    outs = [[] for _ in range(7)]
    for l in range(depth):
        shared = (norm_mix_g[l], w_in[l].astype(BF16), q_norm_g[l], k_norm_g[l],
                  _band_bias(rel_bias_table[l]), gmlp_ln_g[l], gmlp_ln_b[l], gmlp_w_s[l],
                  gmlp_b_s[l], w_out[l].astype(BF16), norm_ffn_g[l],
                  jnp.pad(ffn_conv_w[l], ((0, 0), (0, pad))),
                  jnp.pad(ffn_conv_b[l], ((0, pad),))[None, :])
        xp, kp, vp, _, cp, ffn_bf16 = _layer(xp, seq, None, (w_up[l], w_down[l]), *shared)
        xs, ks, vs, gs, cs, _ = _layer(
            xs, dseq, (cache_attn_k[l], cache_attn_v[l], state_ffn_conv[l]), ffn_bf16, *shared)
        outs[0].append(kp.reshape(batch, keep, N_HEADS, HEAD_DIM))
        outs[1].append(vp.reshape(batch, keep, N_HEADS, HEAD_DIM))
        outs[2].append(cp)
        outs[3].append(ks.reshape(dbatch, dseq, N_HEADS, HEAD_DIM))
        outs[4].append(vs.reshape(dbatch, dseq, N_HEADS, HEAD_DIM))
        outs[5].append(gs.reshape(dbatch, dseq, D_GMLP))
        outs[6].append(cs)
    return (xp.reshape(batch, seq, D_MODEL), xs.reshape(dbatch, dseq, D_MODEL),
            *[jnp.stack(o) for o in outs])
```

```python
import functools

import jax
import jax.numpy as jnp
from jax import lax
from jax.experimental import pallas as pl
from jax.experimental.pallas import tpu as pltpu

D_MODEL = 2048
CHUNK = 64
ATTN_WINDOW = 8 * CHUNK
BAND = ATTN_WINDOW + CHUNK
D_ATTN = D_MODEL // 2
HEAD_DIM = 128
N_HEADS = D_ATTN // HEAD_DIM
D_GMLP = D_MODEL - D_ATTN
N_GROUPS = 8
GROUP_DIM = D_GMLP // N_GROUPS
GMLP_CHUNK = 128
REL_CLIP = 128
D_FF = 5504
CONV_W = 3
D_IN = 3 * D_ATTN + 2 * D_GMLP
EPS = 1e-6
NEG_INF = -1e30
LOG2E = 1.4426950408889634

LANES = 128
SUBLANES = 8
BF16_ROWS = 16
VMEM_LIMIT_BYTES = 60 * 1024 * 1024

UP_SUB = 256
UP_RANGES = ((0, 11 * UP_SUB), (11 * UP_SUB, D_FF - 11 * UP_SUB))
D_FF_PAD = 22 * UP_SUB


def _sub_widths(ncols):
    full, rest = divmod(ncols, UP_SUB)
    return (UP_SUB,) * full + ((rest,) if rest else ())
UP_ROWS = 64
UP_DOT_ROWS = 256
OUT_ROWS = 256
IN_DOT_ROWS = 256
ATTN_GROUP = 4
GQ = ATTN_GROUP * CHUNK
GK = GQ + ATTN_WINDOW
BAND_PAD = BAND + CHUNK
assert BAND_PAD % LANES == 0 and LANES == 2 * CHUNK

F32 = jnp.float32
BF16 = jnp.bfloat16


def _dot(a, b):
    return jnp.dot(a, b, preferred_element_type=F32)


def _gelu(x):
    return 0.5 * x * (1.0 + lax.erf(x * (0.5 ** 0.5)))


def _inproj_kernel(x_ref, nmg_ref, w_ref, qg_ref, kg_ref, lng_ref, lnb_ref,
                   z_ref, kf_ref, vf_ref, *rest, emit_vn):
    x = x_ref[...]
    ms = jnp.mean(x * x, axis=-1, keepdims=True)
    n = (x * lax.rsqrt(ms + EPS) * nmg_ref[...]).astype(BF16)

    def head_rms(zh, g):
        r = lax.rsqrt(jnp.mean(zh * zh, axis=-1, keepdims=True) + EPS)
        return zh * r * g

    sub = 4 * HEAD_DIM
    order = sorted(range(D_IN // sub), key=lambda sb: (4, 3, 0, 1, 2).index(sb * sub // D_ATTN))
    for sb in order:
        seg = (sb * sub) // D_ATTN
        w_sub = w_ref[:, sb * sub:(sb + 1) * sub]
        zs = jnp.concatenate([_dot(n[r:r + IN_DOT_ROWS, :], w_sub)
                              for r in range(0, n.shape[0], IN_DOT_ROWS)], axis=0)
        for hh in range(sub // HEAD_DIM):
            c0 = sb * sub + hh * HEAD_DIM
            lo = c0 - seg * D_ATTN
            zh = zs[:, hh * HEAD_DIM:(hh + 1) * HEAD_DIM]
            if seg == 0:
                out = head_rms(zh, qg_ref[...] * (HEAD_DIM ** -0.5 * LOG2E))
            elif seg == 1:
                out = head_rms(zh, kg_ref[...])
            elif seg == 2:
                out = zh
            elif seg == 3:
                out = _gelu(zh)
            else:
                ge = _gelu(zh)
                mu = jnp.mean(ge, axis=-1, keepdims=True)
                d = ge - mu
                var = jnp.mean(d * d, axis=-1, keepdims=True)
                out = d * lax.rsqrt(var + EPS) * lng_ref[...] + lnb_ref[...]
            z_ref[:, c0:c0 + HEAD_DIM] = out.astype(BF16)
            head_rows = pl.ds(lo // HEAD_DIM, zh.shape[0], stride=N_HEADS)
            if seg == 1:
                kf_ref[head_rows, :] = out
            elif seg == 2:
                vf_ref[head_rows, :] = out
            elif seg == 4 and emit_vn:
                rest[0][:, lo:lo + HEAD_DIM] = out


def _inproj(x2d, nmg, w_in, qg, kg, lng, lnb, *, keep_every, emit_vn):
    m = x2d.shape[0]
    tm = 512
    nt = m // tm
    nkeep = nt // keep_every
    vec = lambda width: pl.BlockSpec((1, width), lambda i: (0, 0))
    keep_spec = pl.BlockSpec((tm * N_HEADS, HEAD_DIM), lambda i: (i // keep_every, 0))
    out_specs = [pl.BlockSpec((tm, D_IN), lambda i: (i, 0)), keep_spec, keep_spec]
    out_shape = [jax.ShapeDtypeStruct((m, D_IN), BF16),
                 jax.ShapeDtypeStruct((nkeep * tm * N_HEADS, HEAD_DIM), F32),
                 jax.ShapeDtypeStruct((nkeep * tm * N_HEADS, HEAD_DIM), F32)]
    if emit_vn:
        out_specs.append(pl.BlockSpec((tm, D_GMLP), lambda i: (i, 0)))
        out_shape.append(jax.ShapeDtypeStruct((m, D_GMLP), F32))
    return pl.pallas_call(
        functools.partial(_inproj_kernel, emit_vn=emit_vn),
        grid=(nt,),
        in_specs=[pl.BlockSpec((tm, D_MODEL), lambda i: (i, 0)),
                  vec(D_MODEL),
                  pl.BlockSpec((D_MODEL, D_IN), lambda i: (0, 0),
                               pipeline_mode=pl.Buffered(1)),
                  vec(HEAD_DIM), vec(HEAD_DIM), vec(GROUP_DIM), vec(GROUP_DIM)],
        out_specs=out_specs,
        out_shape=out_shape,
        compiler_params=pltpu.CompilerParams(
            dimension_semantics=("arbitrary",), vmem_limit_bytes=VMEM_LIMIT_BYTES),
        name="inproj",
    )(x2d, nmg, w_in, qg, kg, lng, lnb)


def _softmax_numer(sc):
    m = jnp.max(sc, axis=-1, keepdims=True)
    return jnp.exp2(sc - m).astype(BF16)


def _pv(p, v):
    ol = _dot(p, v)
    return ol[:, :HEAD_DIM] * (1.0 / ol[:, HEAD_DIM:])


def _qk(q, k):
    return lax.dot_general(q, k, (((1,), (1,)), ((), ())), preferred_element_type=F32)


def _attn_prompt_kernel(q_ref, k_ref, v_ref, bm_ref, wup_ref, wdn_ref, o_ref, *rest):
    *wag_refs, wd_ref, kp_ref, vp_ref = rest
    step = pl.program_id(0) * pl.num_programs(1) + pl.program_id(1)
    _cast_up_rows(wup_ref, wag_refs)
    _cast_down_rows(wdn_ref, wd_ref, step)
    seq = q_ref.shape[0]
    zeros = jnp.zeros((ATTN_WINDOW, HEAD_DIM), BF16)
    kp_ref[0:ATTN_WINDOW, :] = zeros
    vp_ref[0:ATTN_WINDOW, 0:HEAD_DIM] = zeros
    kp_ref[ATTN_WINDOW:, :] = k_ref[...]
    vp_ref[ATTN_WINDOW:, 0:HEAD_DIM] = v_ref[...]
    vp_ref[:, HEAD_DIM:] = jnp.ones((seq + ATTN_WINDOW, HEAD_DIM), BF16)

    def group(q0, pre_stream):
        q = q_ref[pl.ds(q0, GQ), :]
        kb = kp_ref[pl.ds(q0, GK), :]
        vb = vp_ref[pl.ds(q0, GK), :]
        s = _qk(q, kb)
        rows = []
        for c in range(ATTN_GROUP):
            off = (c * CHUNK) // LANES * LANES
            variant = (c * CHUNK - off) // CHUNK
            sc = s[c * CHUNK:(c + 1) * CHUNK, off:off + BAND_PAD] + bm_ref[0, variant]
            if pre_stream:
                col = lax.broadcasted_iota(jnp.int32, (CHUNK, BAND_PAD), 1) + off
                sc = jnp.where(col >= ATTN_WINDOW - q0, sc, NEG_INF)
            parts = [_softmax_numer(sc)]
            if off:
                parts.insert(0, jnp.zeros((CHUNK, off), BF16))
            if GK - off - BAND_PAD:
                parts.append(jnp.zeros((CHUNK, GK - off - BAND_PAD), BF16))
            rows.append(jnp.concatenate(parts, axis=1))
        p = jnp.concatenate(rows, axis=0)
        o_ref[pl.ds(q0, GQ), :] = _pv(p, vb).astype(BF16)

    n_pre = ATTN_WINDOW // GQ
    for gi in range(n_pre):
        group(gi * GQ, True)

    def body(gi, carry):
        group(pl.multiple_of(gi * GQ, GQ), False)
        return carry

    lax.fori_loop(n_pre, seq // GQ, body, 0, unroll=True)


def _attn_prompt(z, bm, w_up, w_down, *, batch, seq):
    steps = batch * N_HEADS
    up_rows = D_MODEL // steps
    down_rows = -(-D_FF_PAD // (steps * BF16_ROWS)) * BF16_ROWS
    assert up_rows * steps == D_MODEL and up_rows % BF16_ROWS == 0
    last_down = (D_FF - 1) // down_rows
    step = lambda b, h: b * N_HEADS + h
    row_block = lambda width: pl.BlockSpec((up_rows, width), lambda b, h: (step(b, h), 0))
    oa, *wags, wd = pl.pallas_call(
        _attn_prompt_kernel,
        grid=(batch, N_HEADS),
        in_specs=[pl.BlockSpec((seq, HEAD_DIM), lambda b, h: (b, h)),
                  pl.BlockSpec((seq, HEAD_DIM), lambda b, h: (b, N_HEADS + h)),
                  pl.BlockSpec((seq, HEAD_DIM), lambda b, h: (b, 2 * N_HEADS + h)),
                  pl.BlockSpec((1, 2, CHUNK, BAND_PAD), lambda b, h: (h, 0, 0, 0)),
                  row_block(2 * D_FF),
                  pl.BlockSpec((down_rows, D_MODEL),
                               lambda b, h: (jnp.minimum(step(b, h), last_down), 0))],
        out_specs=[pl.BlockSpec((seq, HEAD_DIM), lambda b, h: (b, h)),
                   *[row_block(2 * nc) for _, nc in UP_RANGES],
                   pl.BlockSpec((down_rows, D_MODEL), lambda b, h: (step(b, h), 0))],
        out_shape=[jax.ShapeDtypeStruct((batch * seq, D_ATTN), BF16),
                   *[jax.ShapeDtypeStruct((D_MODEL, 2 * nc), BF16) for _, nc in UP_RANGES],
                   jax.ShapeDtypeStruct((steps * down_rows, D_MODEL), BF16)],
        scratch_shapes=[pltpu.VMEM((seq + ATTN_WINDOW, HEAD_DIM), BF16),
                        pltpu.VMEM((seq + ATTN_WINDOW, 2 * HEAD_DIM), BF16)],
        compiler_params=pltpu.CompilerParams(
            dimension_semantics=("arbitrary", "arbitrary"),
            vmem_limit_bytes=VMEM_LIMIT_BYTES),
        name="attn_prompt",
    )(z, z, z, bm, w_up, w_down)
    return oa, wags, wd


def _attn_sample_kernel(q_ref, kn_ref, vn_ref, kc_ref, vc_ref, bias_ref, o_ref):
    w = kc_ref.shape[0] // N_HEADS
    ones = jnp.ones((w + kn_ref.shape[0], HEAD_DIM), BF16)
    for h in range(N_HEADS):
        hc = slice(h * HEAD_DIM, (h + 1) * HEAD_DIM)
        head_rows = pl.ds(h, w, stride=N_HEADS)
        q = q_ref[:, hc]
        s = jnp.concatenate([_qk(q, kc_ref[head_rows, :].astype(BF16)),
                             _qk(q, kn_ref[:, hc])], axis=1)
        v = jnp.concatenate([vc_ref[head_rows, :].astype(BF16), vn_ref[:, hc]], axis=0)
        o = _pv(_softmax_numer(s + bias_ref[h]), jnp.concatenate([v, ones], axis=1))
        o_ref[:, hc] = o.astype(BF16)


def _attn_sample(z, k_cache, v_cache, bias, *, batch, t):
    w = k_cache.shape[0] // (batch * N_HEADS)
    return pl.pallas_call(
        _attn_sample_kernel,
        grid=(batch,),
        in_specs=[pl.BlockSpec((t, D_ATTN), lambda b: (b, 0)),
                  pl.BlockSpec((t, D_ATTN), lambda b: (b, 1)),
                  pl.BlockSpec((t, D_ATTN), lambda b: (b, 2)),
                  pl.BlockSpec((w * N_HEADS, HEAD_DIM), lambda b: (b, 0)),
                  pl.BlockSpec((w * N_HEADS, HEAD_DIM), lambda b: (b, 0)),
                  pl.BlockSpec((N_HEADS, t, w + t), lambda b: (0, 0, 0))],
        out_specs=pl.BlockSpec((t, D_ATTN), lambda b: (b, 0)),
        out_shape=jax.ShapeDtypeStruct((batch * t, D_ATTN), BF16),
        compiler_params=pltpu.CompilerParams(
            dimension_semantics=("arbitrary",), vmem_limit_bytes=VMEM_LIMIT_BYTES),
        name="attn_sample",
    )(z, z, z, k_cache, v_cache, bias)


def _outproj_kernel(oa_ref, u_ref, vn_ref, x_ref, wout_ref, ws_ref, bsb_ref, g_ref,
                    h_ref, n2_ref, *, chunk):
    tm = x_ref.shape[0]
    row = lax.broadcasted_iota(jnp.int32, (chunk, chunk), 0)
    col = lax.broadcasted_iota(jnp.int32, (chunk, chunk), 1)
    tri = row >= col
    wsg = [jnp.where(tri, ws_ref[g], 0.0).astype(BF16) for g in range(N_GROUPS)]
    nck = OUT_ROWS // chunk
    for r0 in range(0, tm, OUT_ROWS):
        ob = [[None] * N_GROUPS for _ in range(nck)]
        for g in range(N_GROUPS):
            gc = slice(g * GROUP_DIM, (g + 1) * GROUP_DIM)
            vn = jnp.concatenate([vn_ref[r0 + c * chunk:r0 + (c + 1) * chunk, gc]
                                  for c in range(nck)], axis=1)
            vs = _dot(wsg[g], vn)
            for c in range(nck):
                rows = slice(r0 + c * chunk, r0 + (c + 1) * chunk)
                gate = vs[:, c * GROUP_DIM:(c + 1) * GROUP_DIM] + bsb_ref[g]
                ob[c][g] = (u_ref[rows, gc].astype(F32) * gate).astype(BF16)
        ob = jnp.concatenate([jnp.concatenate(obc, axis=1) for obc in ob], axis=0)
        rows = slice(r0, r0 + OUT_ROWS)
        h = (x_ref[rows, :] + _dot(oa_ref[rows, :], wout_ref[0:D_ATTN, :])
             + _dot(ob, wout_ref[D_ATTN:, :]))
        h_ref[rows, :] = h
        ms = jnp.mean(h * h, axis=-1, keepdims=True)
        n2_ref[rows, :] = (h * lax.rsqrt(ms + EPS) * g_ref[...]).astype(BF16)


def _outproj(oa, z, x2d, w_out, ws, bsb, nfg, *, chunk):
    m = x2d.shape[0]
    tm = 512
    return pl.pallas_call(
        functools.partial(_outproj_kernel, chunk=chunk),
        grid=(m // tm,),
        in_specs=[pl.BlockSpec((tm, D_ATTN), lambda i: (i, 0)),
                  pl.BlockSpec((tm, D_GMLP), lambda i: (i, 3)),
                  pl.BlockSpec((tm, D_GMLP), lambda i: (i, 4)),
                  pl.BlockSpec((tm, D_MODEL), lambda i: (i, 0)),
                  pl.BlockSpec((D_MODEL, D_MODEL), lambda i: (0, 0),
                               pipeline_mode=pl.Buffered(1)),
                  pl.BlockSpec((N_GROUPS, chunk, chunk), lambda i: (0, 0, 0)),
                  pl.BlockSpec((N_GROUPS, chunk, GROUP_DIM), lambda i: (0, 0, 0)),
                  pl.BlockSpec((1, D_MODEL), lambda i: (0, 0))],
        out_specs=[pl.BlockSpec((tm, D_MODEL), lambda i: (i, 0)),
                   pl.BlockSpec((tm, D_MODEL), lambda i: (i, 0))],
        out_shape=[jax.ShapeDtypeStruct((m, D_MODEL), F32),
                   jax.ShapeDtypeStruct((m, D_MODEL), BF16)],
        compiler_params=pltpu.CompilerParams(
            dimension_semantics=("arbitrary",), vmem_limit_bytes=VMEM_LIMIT_BYTES),
        name="outproj",
    )(oa, z, z, x2d, w_out, ws, bsb, nfg)


def _silu(x):
    return x * (1.0 / (1.0 + jnp.exp(-x)))


def _up_kernel(n2_ref, w_ref, cw_ref, cb_ref, st_ref, m_ref, cs_ref,
               carry_ref, *, nseg, tiles_per_seq):
    i = pl.program_id(0)
    tm = n2_ref.shape[0]
    sl = tm // nseg
    hist = SUBLANES
    n2 = n2_ref[...]
    off = 0
    for sub in _sub_widths(m_ref.shape[1]):
        cs = slice(off, off + sub)
        w_sub = w_ref[:, 2 * off:2 * (off + sub)]
        off += sub
        ag = jnp.concatenate([_dot(n2[r:r + UP_DOT_ROWS, :], w_sub)
                              for r in range(0, tm, UP_DOT_ROWS)], axis=0)
        a = ag[:, :sub]
        gate = ag[:, sub:]
        w0 = cw_ref[0:1, cs]
        w1 = cw_ref[1:2, cs]
        w2 = cw_ref[2:3, cs]
        cb = cb_ref[:, cs]
        for r0 in range(0, tm, UP_ROWS):
            if r0 % sl:
                prev = a[r0 - hist:r0, :]
            elif nseg == 1:
                first = (i % tiles_per_seq) == 0
                prev = jnp.where(first, 0.0, carry_ref[:, cs])
            else:
                prev = st_ref[r0 // sl, :, cs]
            ac_rows = a[r0:r0 + UP_ROWS, :]
            win = jnp.concatenate([prev, ac_rows], axis=0)
            a_m1 = pltpu.roll(win, 1, 0)[hist:, :]
            a_m2 = pltpu.roll(win, 2, 0)[hist:, :]
            act = _silu(cb + w0 * a_m2 + w1 * a_m1 + w2 * ac_rows)
            m_ref[r0:r0 + UP_ROWS, cs] = (act * gate[r0:r0 + UP_ROWS, :]).astype(BF16)
        for s in range(nseg):
            tail = a[(s + 1) * sl - hist:(s + 1) * sl, :]
            if nseg == 1:
                carry_ref[:, cs] = tail
                cs_ref[0, :, cs] = tail
            else:
                cs_ref[s, :, cs] = tail


def _up(n2, wag, cw, cb, state8, *, nseg, tiles_per_seq):
    m = n2.shape[0]
    ncols = cw.shape[1]
    tm = 1024
    nt = m // tm
    if nseg == 1:
        cs_rows = nt // tiles_per_seq
        cs_spec = pl.BlockSpec((1, SUBLANES, ncols), lambda i: (i // tiles_per_seq, 0, 0))
        st_spec = pl.BlockSpec((1, SUBLANES, ncols), lambda i: (0, 0, 0))
    else:
        cs_rows = nt * nseg
        cs_spec = pl.BlockSpec((nseg, SUBLANES, ncols), lambda i: (i, 0, 0))
        st_spec = pl.BlockSpec((nseg, SUBLANES, ncols), lambda i: (i, 0, 0))
    return pl.pallas_call(
        functools.partial(_up_kernel, nseg=nseg, tiles_per_seq=tiles_per_seq),
        grid=(nt,),
        in_specs=[pl.BlockSpec((tm, D_MODEL), lambda i: (i, 0)),
                  pl.BlockSpec((D_MODEL, 2 * ncols), lambda i: (0, 0),
                               pipeline_mode=pl.Buffered(1)),
                  pl.BlockSpec((CONV_W, ncols), lambda i: (0, 0)),
                  pl.BlockSpec((1, ncols), lambda i: (0, 0)),
                  st_spec],
        out_specs=[pl.BlockSpec((tm, ncols), lambda i: (i, 0)), cs_spec],
        out_shape=[jax.ShapeDtypeStruct((m, ncols), BF16),
                   jax.ShapeDtypeStruct((cs_rows, SUBLANES, ncols), F32)],
        scratch_shapes=[pltpu.VMEM((SUBLANES, ncols), F32)],
        compiler_params=pltpu.CompilerParams(
            dimension_semantics=("arbitrary",), vmem_limit_bytes=VMEM_LIMIT_BYTES),
        name="up",
    )(n2, wag, cw, cb, state8)


def _down_kernel(*refs):
    *m_refs, w_ref, h_ref, y_ref = refs
    y = h_ref[...]
    for (col0, ncols), m_ref in zip(UP_RANGES, m_refs):
        y = y + _dot(m_ref[...], w_ref[col0:col0 + ncols, :])
    y_ref[...] = y


def _down(ms, w_down, h):
    m = h.shape[0]
    tm = 512
    return pl.pallas_call(
        _down_kernel,
        grid=(m // tm,),
        in_specs=[*[pl.BlockSpec((tm, nc), lambda i: (i, 0)) for _, nc in UP_RANGES],
                  pl.BlockSpec((D_FF_PAD, D_MODEL), lambda i: (0, 0),
                               pipeline_mode=pl.Buffered(1)),
                  pl.BlockSpec((tm, D_MODEL), lambda i: (i, 0))],
        out_specs=pl.BlockSpec((tm, D_MODEL), lambda i: (i, 0)),
        out_shape=jax.ShapeDtypeStruct((m, D_MODEL), F32),
        compiler_params=pltpu.CompilerParams(
            dimension_semantics=("arbitrary",), vmem_limit_bytes=VMEM_LIMIT_BYTES),
        name="down",
    )(*ms, w_down, h)


def _band_bias(table):
    d = jnp.arange(-(CHUNK - 1), BAND)
    e = table[:, jnp.clip(ATTN_WINDOW - d, -REL_CLIP, REL_CLIP) + REL_CLIP].astype(F32)
    n = e.shape[1]
    ep = jnp.pad(e, ((0, 0), (0, 1)))
    toep = jnp.tile(ep, (1, CHUNK))[:, :CHUNK * n].reshape(N_HEADS, CHUNK, n)
    return toep[:, :, CHUNK - 1:CHUNK - 1 + BAND] * LOG2E


def _group_bias(bias):
    pad = BAND_PAD - BAND
    even = jnp.pad(bias, ((0, 0), (0, 0), (0, pad)), constant_values=NEG_INF)
    odd = jnp.pad(bias, ((0, 0), (0, 0), (pad, 0)), constant_values=NEG_INF)
    return jnp.stack([even, odd], axis=1)


def _cast_up_rows(w_ref, o_refs):
    for (col0, ncols), o_ref in zip(UP_RANGES, o_refs):
        off = 0
        for width in _sub_widths(ncols):
            for half in range(2):
                src = half * D_FF + col0 + off
                dst = 2 * off + half * width
                o_ref[:, dst:dst + width] = w_ref[:, src:src + width].astype(BF16)
            off += width


def _cast_down_rows(w_ref, o_ref, block):
    tr = w_ref.shape[0]
    row = lax.broadcasted_iota(jnp.int32, w_ref.shape, 0) + block * tr
    o_ref[...] = jnp.where(row < D_FF, w_ref[...], 0.0).astype(BF16)


def _layer(x2d, seq, sample_cache, ffn_weights, nmg, w_in, qg, kg, bias, lng, lnb, ws, bs,
           w_out, nfg, cw, cb):
    m = x2d.shape[0]
    batch = m // seq
    is_sample = sample_cache is not None
    row = lambda v: v[None, :]
    if is_sample:
        wags, wd = ffn_weights
        z, kf, vf, vnf = _inproj(x2d, row(nmg), w_in, row(qg), row(kg), row(lng), row(lnb),
                                 keep_every=1, emit_vn=True)
        ck, cv, cst = sample_cache
        w_cache = ck.shape[1]
        oa = _attn_sample(z, ck.reshape(batch * w_cache * N_HEADS, HEAD_DIM),
                          cv.reshape(batch * w_cache * N_HEADS, HEAD_DIM),
                          bias[:, :, BAND - w_cache - seq:], batch=batch, t=seq)
        chunk = seq
        state8 = jnp.pad(cst, ((0, 0), (SUBLANES - (CONV_W - 1), 0), (0, 0)))
        nseg, tiles_per_seq = batch, 1
    else:
        w_up, w_down = ffn_weights
        keep_every = seq // 512
        z, kf, vf = _inproj(x2d, row(nmg), w_in, row(qg), row(kg), row(lng), row(lnb),
                            keep_every=keep_every, emit_vn=False)
        vnf = None
        oa, wags, wd = _attn_prompt(z, _group_bias(bias), w_up, w_down, batch=batch, seq=seq)
        chunk = GMLP_CHUNK
        state8 = jnp.zeros((1, SUBLANES, D_FF), F32)
        nseg, tiles_per_seq = 1, seq // 1024
    wsl = ws[:, :chunk, :chunk]
    bsb = jnp.broadcast_to(bs[:, :chunk, None], (N_GROUPS, chunk, GROUP_DIM))
    h, n2 = _outproj(oa, z, x2d, w_out, wsl, bsb, row(nfg), chunk=chunk)
    ms, tails = zip(*[
        _up(n2, wag, cw[:, c0:c0 + nc], cb[:, c0:c0 + nc], state8[:, :, c0:c0 + nc],
            nseg=nseg, tiles_per_seq=tiles_per_seq)
        for (c0, nc), wag in zip(UP_RANGES, wags)])
    y = _down(ms, wd, h)
    conv_state = jnp.concatenate(tails, axis=2)[:, SUBLANES - (CONV_W - 1):, :]
    return y, kf, vf, vnf, conv_state, (wags, wd)


def kernel(x_prompt, x_sample, cache_attn_k, cache_attn_v, state_ffn_conv, norm_mix_g, w_in,
           q_norm_g, k_norm_g, rel_bias_table, gmlp_ln_g, gmlp_ln_b, gmlp_w_s, gmlp_b_s, w_out,
           norm_ffn_g, w_up, ffn_conv_w, ffn_conv_b, w_down):
    batch, seq, _ = x_prompt.shape
    dbatch, dseq, _ = x_sample.shape
    depth = w_in.shape[0]
    xp = x_prompt.reshape(batch * seq, D_MODEL)
    xs = x_sample.reshape(dbatch * dseq, D_MODEL)
    keep = min(ATTN_WINDOW, seq)
    outs = [[] for _ in range(7)]
    for l in range(depth):
        shared = (norm_mix_g[l], w_in[l].astype(BF16), q_norm_g[l], k_norm_g[l],
                  _band_bias(rel_bias_table[l]), gmlp_ln_g[l], gmlp_ln_b[l], gmlp_w_s[l],
                  gmlp_b_s[l], w_out[l].astype(BF16), norm_ffn_g[l],
                  ffn_conv_w[l], ffn_conv_b[l][None, :])
        xp, kp, vp, _, cp, ffn_bf16 = _layer(xp, seq, None, (w_up[l], w_down[l]), *shared)
        xs, ks, vs, gs, cs, _ = _layer(
            xs, dseq, (cache_attn_k[l], cache_attn_v[l], state_ffn_conv[l]), ffn_bf16, *shared)
        outs[0].append(kp.reshape(batch, keep, N_HEADS, HEAD_DIM))
        outs[1].append(vp.reshape(batch, keep, N_HEADS, HEAD_DIM))
        outs[2].append(cp)
        outs[3].append(ks.reshape(dbatch, dseq, N_HEADS, HEAD_DIM))
        outs[4].append(vs.reshape(dbatch, dseq, N_HEADS, HEAD_DIM))
        outs[5].append(gs.reshape(dbatch, dseq, D_GMLP))
        outs[6].append(cs)
    return (xp.reshape(batch, seq, D_MODEL), xs.reshape(dbatch, dseq, D_MODEL),
            *[jnp.stack(o) for o in outs])
```

```python
import functools

import jax
import jax.numpy as jnp
from jax import lax
from jax.experimental import pallas as pl
from jax.experimental.pallas import tpu as pltpu

D_MODEL = 2048
CHUNK = 64
ATTN_WINDOW = 8 * CHUNK
BAND = ATTN_WINDOW + CHUNK
D_ATTN = D_MODEL // 2
HEAD_DIM = 128
N_HEADS = D_ATTN // HEAD_DIM
D_GMLP = D_MODEL - D_ATTN
N_GROUPS = 8
GROUP_DIM = D_GMLP // N_GROUPS
GMLP_CHUNK = 128
REL_CLIP = 128
D_FF = 5504
CONV_W = 3
D_IN = 3 * D_ATTN + 2 * D_GMLP
EPS = 1e-6
NEG_INF = -1e30
LOG2E = 1.4426950408889634

LANES = 128
SUBLANES = 8
BF16_ROWS = 16
VMEM_LIMIT_BYTES = 60 * 1024 * 1024

UP_SUB = 256
UP_RANGES = ((0, 11 * UP_SUB), (11 * UP_SUB, D_FF - 11 * UP_SUB))
D_FF_PAD = 22 * UP_SUB


def _sub_widths(ncols):
    full, rest = divmod(ncols, UP_SUB)
    return (UP_SUB,) * full + ((rest,) if rest else ())
UP_ROWS = 64
UP_DOT_ROWS = 256
OUT_ROWS = 256
IN_DOT_ROWS = 256
ATTN_GROUP = 4
ATTN_HEADS_PER_STEP = 2
GQ = ATTN_GROUP * CHUNK
GK = GQ + ATTN_WINDOW
BAND_PAD = BAND + CHUNK
assert BAND_PAD % LANES == 0 and LANES == 2 * CHUNK

F32 = jnp.float32
BF16 = jnp.bfloat16


def _dot(a, b):
    return jnp.dot(a, b, preferred_element_type=F32)


def _gelu(x):
    return 0.5 * x * (1.0 + lax.erf(x * (0.5 ** 0.5)))


def _inproj_kernel(x_ref, nmg_ref, w_ref, qg_ref, kg_ref, lng_ref, lnb_ref,
                   z_ref, kf_ref, vf_ref, *rest, emit_vn):
    x = x_ref[...]
    ms = jnp.mean(x * x, axis=-1, keepdims=True)
    n = (x * lax.rsqrt(ms + EPS) * nmg_ref[...]).astype(BF16)

    def head_rms(zh, g):
        r = lax.rsqrt(jnp.mean(zh * zh, axis=-1, keepdims=True) + EPS)
        return zh * r * g

    sub = 4 * HEAD_DIM
    order = sorted(range(D_IN // sub), key=lambda sb: (4, 3, 0, 1, 2).index(sb * sub // D_ATTN))
    for sb in order:
        seg = (sb * sub) // D_ATTN
        w_sub = w_ref[:, sb * sub:(sb + 1) * sub]
        zs = jnp.concatenate([_dot(n[r:r + IN_DOT_ROWS, :], w_sub)
                              for r in range(0, n.shape[0], IN_DOT_ROWS)], axis=0)
        for hh in range(sub // HEAD_DIM):
            c0 = sb * sub + hh * HEAD_DIM
            lo = c0 - seg * D_ATTN
            zh = zs[:, hh * HEAD_DIM:(hh + 1) * HEAD_DIM]
            if seg == 0:
                out = head_rms(zh, qg_ref[...] * (HEAD_DIM ** -0.5 * LOG2E))
            elif seg == 1:
                out = head_rms(zh, kg_ref[...])
            elif seg == 2:
                out = zh
            elif seg == 3:
                out = _gelu(zh)
            else:
                ge = _gelu(zh)
                mu = jnp.mean(ge, axis=-1, keepdims=True)
                d = ge - mu
                var = jnp.mean(d * d, axis=-1, keepdims=True)
                out = d * lax.rsqrt(var + EPS) * lng_ref[...] + lnb_ref[...]
            z_ref[:, c0:c0 + HEAD_DIM] = out.astype(BF16)
            head_rows = pl.ds(lo // HEAD_DIM, zh.shape[0], stride=N_HEADS)
            if seg == 1:
                kf_ref[head_rows, :] = out
            elif seg == 2:
                vf_ref[head_rows, :] = out
            elif seg == 4 and emit_vn:
                rest[0][:, lo:lo + HEAD_DIM] = out


def _inproj(x2d, nmg, w_in, qg, kg, lng, lnb, *, keep_every, emit_vn):
    m = x2d.shape[0]
    tm = 512
    nt = m // tm
    nkeep = nt // keep_every
    vec = lambda width: pl.BlockSpec((1, width), lambda i: (0, 0))
    keep_spec = pl.BlockSpec((tm * N_HEADS, HEAD_DIM), lambda i: (i // keep_every, 0))
    out_specs = [pl.BlockSpec((tm, D_IN), lambda i: (i, 0)), keep_spec, keep_spec]
    out_shape = [jax.ShapeDtypeStruct((m, D_IN), BF16),
                 jax.ShapeDtypeStruct((nkeep * tm * N_HEADS, HEAD_DIM), F32),
                 jax.ShapeDtypeStruct((nkeep * tm * N_HEADS, HEAD_DIM), F32)]
    if emit_vn:
        out_specs.append(pl.BlockSpec((tm, D_GMLP), lambda i: (i, 0)))
        out_shape.append(jax.ShapeDtypeStruct((m, D_GMLP), F32))
    return pl.pallas_call(
        functools.partial(_inproj_kernel, emit_vn=emit_vn),
        grid=(nt,),
        in_specs=[pl.BlockSpec((tm, D_MODEL), lambda i: (i, 0)),
                  vec(D_MODEL),
                  pl.BlockSpec((D_MODEL, D_IN), lambda i: (0, 0),
                               pipeline_mode=pl.Buffered(1)),
                  vec(HEAD_DIM), vec(HEAD_DIM), vec(GROUP_DIM), vec(GROUP_DIM)],
        out_specs=out_specs,
        out_shape=out_shape,
        compiler_params=pltpu.CompilerParams(
            dimension_semantics=("arbitrary",), vmem_limit_bytes=VMEM_LIMIT_BYTES),
        name="inproj",
    )(x2d, nmg, w_in, qg, kg, lng, lnb)


def _softmax_numer(sc):
    m = jnp.max(sc, axis=-1, keepdims=True)
    return jnp.exp2(sc - m).astype(BF16)


def _pv(p, v):
    ol = _dot(p, v)
    return ol[:, :HEAD_DIM] * (1.0 / ol[:, HEAD_DIM:])


def _qk(q, k):
    return lax.dot_general(q, k, (((1,), (1,)), ((), ())), preferred_element_type=F32)


def _attn_prompt_kernel(q_ref, k_ref, v_ref, bm_ref, wup_ref, wdn_ref, o_ref, *rest):
    *wag_refs, wd_ref, kp_ref, vp_ref = rest
    step = pl.program_id(0) * pl.num_programs(1) + pl.program_id(1)
    _cast_up_rows(wup_ref, wag_refs)
    _cast_down_rows(wdn_ref, wd_ref, step)
    seq = q_ref.shape[0]
    heads = [slice(hd * HEAD_DIM, (hd + 1) * HEAD_DIM) for hd in range(ATTN_HEADS_PER_STEP)]
    zeros = jnp.zeros((ATTN_WINDOW, HEAD_DIM), BF16)
    for hd, hc in enumerate(heads):
        kp_ref[hd, 0:ATTN_WINDOW, :] = zeros
        vp_ref[hd, 0:ATTN_WINDOW, 0:HEAD_DIM] = zeros
        kp_ref[hd, ATTN_WINDOW:, :] = k_ref[:, hc]
        vp_ref[hd, ATTN_WINDOW:, 0:HEAD_DIM] = v_ref[:, hc]
        vp_ref[hd, :, HEAD_DIM:] = jnp.ones((seq + ATTN_WINDOW, HEAD_DIM), BF16)

    def group(hd, q0, pre_stream):
        q = q_ref[pl.ds(q0, GQ), heads[hd]]
        kb = kp_ref[hd, pl.ds(q0, GK), :]
        vb = vp_ref[hd, pl.ds(q0, GK), :]
        s = _qk(q, kb)
        rows = []
        for c in range(ATTN_GROUP):
            off = (c * CHUNK) // LANES * LANES
            variant = (c * CHUNK - off) // CHUNK
            sc = s[c * CHUNK:(c + 1) * CHUNK, off:off + BAND_PAD] + bm_ref[hd, variant]
            if pre_stream:
                col = lax.broadcasted_iota(jnp.int32, (CHUNK, BAND_PAD), 1) + off
                sc = jnp.where(col >= ATTN_WINDOW - q0, sc, NEG_INF)
            parts = [_softmax_numer(sc)]
            if off:
                parts.insert(0, jnp.zeros((CHUNK, off), BF16))
            if GK - off - BAND_PAD:
                parts.append(jnp.zeros((CHUNK, GK - off - BAND_PAD), BF16))
            rows.append(jnp.concatenate(parts, axis=1))
        p = jnp.concatenate(rows, axis=0)
        o_ref[pl.ds(q0, GQ), heads[hd]] = _pv(p, vb).astype(BF16)

    n_pre = ATTN_WINDOW // GQ
    for gi in range(n_pre):
        for hd in range(ATTN_HEADS_PER_STEP):
            group(hd, gi * GQ, True)

    def body(gi, carry):
        for hd in range(ATTN_HEADS_PER_STEP):
            group(hd, pl.multiple_of(gi * GQ, GQ), False)
        return carry

    lax.fori_loop(n_pre, seq // GQ, body, 0, unroll=True)


def _attn_prompt(z, bm, w_up, w_down, *, batch, seq):
    hp = ATTN_HEADS_PER_STEP
    nh = N_HEADS // hp
    steps = batch * nh
    up_rows = D_MODEL // steps
    down_rows = -(-D_FF_PAD // (steps * BF16_ROWS)) * BF16_ROWS
    assert up_rows * steps == D_MODEL and up_rows % BF16_ROWS == 0
    last_down = (D_FF - 1) // down_rows
    step = lambda b, h: b * nh + h
    row_block = lambda width: pl.BlockSpec((up_rows, width), lambda b, h: (step(b, h), 0))
    head_block = lambda seg: pl.BlockSpec((seq, hp * HEAD_DIM), lambda b, h: (b, seg * nh + h))
    oa, *wags, wd = pl.pallas_call(
        _attn_prompt_kernel,
        grid=(batch, nh),
        in_specs=[head_block(0), head_block(1), head_block(2),
                  pl.BlockSpec((hp, 2, CHUNK, BAND_PAD), lambda b, h: (h, 0, 0, 0)),
                  row_block(2 * D_FF),
                  pl.BlockSpec((down_rows, D_MODEL),
                               lambda b, h: (jnp.minimum(step(b, h), last_down), 0))],
        out_specs=[head_block(0),
                   *[row_block(2 * nc) for _, nc in UP_RANGES],
                   pl.BlockSpec((down_rows, D_MODEL), lambda b, h: (step(b, h), 0))],
        out_shape=[jax.ShapeDtypeStruct((batch * seq, D_ATTN), BF16),
                   *[jax.ShapeDtypeStruct((D_MODEL, 2 * nc), BF16) for _, nc in UP_RANGES],
                   jax.ShapeDtypeStruct((steps * down_rows, D_MODEL), BF16)],
        scratch_shapes=[pltpu.VMEM((hp, seq + ATTN_WINDOW, HEAD_DIM), BF16),
                        pltpu.VMEM((hp, seq + ATTN_WINDOW, 2 * HEAD_DIM), BF16)],
        compiler_params=pltpu.CompilerParams(
            dimension_semantics=("arbitrary", "arbitrary"),
            vmem_limit_bytes=VMEM_LIMIT_BYTES),
        name="attn_prompt",
    )(z, z, z, bm, w_up, w_down)
    return oa, wags, wd


def _attn_sample_kernel(q_ref, kn_ref, vn_ref, kc_ref, vc_ref, bias_ref, o_ref):
    w = kc_ref.shape[0] // N_HEADS
    ones = jnp.ones((w + kn_ref.shape[0], HEAD_DIM), BF16)
    for h in range(N_HEADS):
        hc = slice(h * HEAD_DIM, (h + 1) * HEAD_DIM)
        head_rows = pl.ds(h, w, stride=N_HEADS)
        q = q_ref[:, hc]
        s = jnp.concatenate([_qk(q, kc_ref[head_rows, :].astype(BF16)),
                             _qk(q, kn_ref[:, hc])], axis=1)
        v = jnp.concatenate([vc_ref[head_rows, :].astype(BF16), vn_ref[:, hc]], axis=0)
        o = _pv(_softmax_numer(s + bias_ref[h]), jnp.concatenate([v, ones], axis=1))
        o_ref[:, hc] = o.astype(BF16)


def _attn_sample(z, k_cache, v_cache, bias, *, batch, t):
    w = k_cache.shape[0] // (batch * N_HEADS)
    return pl.pallas_call(
        _attn_sample_kernel,
        grid=(batch,),
        in_specs=[pl.BlockSpec((t, D_ATTN), lambda b: (b, 0)),
                  pl.BlockSpec((t, D_ATTN), lambda b: (b, 1)),
                  pl.BlockSpec((t, D_ATTN), lambda b: (b, 2)),
                  pl.BlockSpec((w * N_HEADS, HEAD_DIM), lambda b: (b, 0)),
                  pl.BlockSpec((w * N_HEADS, HEAD_DIM), lambda b: (b, 0)),
                  pl.BlockSpec((N_HEADS, t, w + t), lambda b: (0, 0, 0))],
        out_specs=pl.BlockSpec((t, D_ATTN), lambda b: (b, 0)),
        out_shape=jax.ShapeDtypeStruct((batch * t, D_ATTN), BF16),
        compiler_params=pltpu.CompilerParams(
            dimension_semantics=("arbitrary",), vmem_limit_bytes=VMEM_LIMIT_BYTES),
        name="attn_sample",
    )(z, z, z, k_cache, v_cache, bias)


def _outproj_kernel(oa_ref, u_ref, vn_ref, x_ref, wout_ref, ws_ref, bsb_ref, g_ref,
                    h_ref, n2_ref, *, chunk):
    tm = x_ref.shape[0]
    row = lax.broadcasted_iota(jnp.int32, (chunk, chunk), 0)
    col = lax.broadcasted_iota(jnp.int32, (chunk, chunk), 1)
    tri = row >= col
    wsg = [jnp.where(tri, ws_ref[g], 0.0).astype(BF16) for g in range(N_GROUPS)]
    nck = OUT_ROWS // chunk
    for r0 in range(0, tm, OUT_ROWS):
        ob = [[None] * N_GROUPS for _ in range(nck)]
        for g in range(N_GROUPS):
            gc = slice(g * GROUP_DIM, (g + 1) * GROUP_DIM)
            vn = jnp.concatenate([vn_ref[r0 + c * chunk:r0 + (c + 1) * chunk, gc]
                                  for c in range(nck)], axis=1)
            vs = _dot(wsg[g], vn)
            for c in range(nck):
                rows = slice(r0 + c * chunk, r0 + (c + 1) * chunk)
                gate = vs[:, c * GROUP_DIM:(c + 1) * GROUP_DIM] + bsb_ref[g]
                ob[c][g] = (u_ref[rows, gc].astype(F32) * gate).astype(BF16)
        ob = jnp.concatenate([jnp.concatenate(obc, axis=1) for obc in ob], axis=0)
        rows = slice(r0, r0 + OUT_ROWS)
        h = (x_ref[rows, :] + _dot(oa_ref[rows, :], wout_ref[0:D_ATTN, :])
             + _dot(ob, wout_ref[D_ATTN:, :]))
        h_ref[rows, :] = h
        ms = jnp.mean(h * h, axis=-1, keepdims=True)
        n2_ref[rows, :] = (h * lax.rsqrt(ms + EPS) * g_ref[...]).astype(BF16)


def _outproj(oa, z, x2d, w_out, ws, bsb, nfg, *, chunk):
    m = x2d.shape[0]
    tm = 512
    return pl.pallas_call(
        functools.partial(_outproj_kernel, chunk=chunk),
        grid=(m // tm,),
        in_specs=[pl.BlockSpec((tm, D_ATTN), lambda i: (i, 0)),
                  pl.BlockSpec((tm, D_GMLP), lambda i: (i, 3)),
                  pl.BlockSpec((tm, D_GMLP), lambda i: (i, 4)),
                  pl.BlockSpec((tm, D_MODEL), lambda i: (i, 0)),
                  pl.BlockSpec((D_MODEL, D_MODEL), lambda i: (0, 0),
                               pipeline_mode=pl.Buffered(1)),
                  pl.BlockSpec((N_GROUPS, chunk, chunk), lambda i: (0, 0, 0)),
                  pl.BlockSpec((N_GROUPS, chunk, GROUP_DIM), lambda i: (0, 0, 0)),
                  pl.BlockSpec((1, D_MODEL), lambda i: (0, 0))],
        out_specs=[pl.BlockSpec((tm, D_MODEL), lambda i: (i, 0)),
                   pl.BlockSpec((tm, D_MODEL), lambda i: (i, 0))],
        out_shape=[jax.ShapeDtypeStruct((m, D_MODEL), F32),
                   jax.ShapeDtypeStruct((m, D_MODEL), BF16)],
        compiler_params=pltpu.CompilerParams(
            dimension_semantics=("arbitrary",), vmem_limit_bytes=VMEM_LIMIT_BYTES),
        name="outproj",
    )(oa, z, z, x2d, w_out, ws, bsb, nfg)


def _silu(x):
    return x * (1.0 / (1.0 + jnp.exp(-x)))


def _up_kernel(n2_ref, w_ref, cw_ref, cb_ref, st_ref, m_ref, cs_ref,
               carry_ref, *, nseg, tiles_per_seq):
    i = pl.program_id(0)
    tm = n2_ref.shape[0]
    sl = tm // nseg
    hist = SUBLANES
    n2 = n2_ref[...]
    off = 0
    for sub in _sub_widths(m_ref.shape[1]):
        cs = slice(off, off + sub)
        w_sub = w_ref[:, 2 * off:2 * (off + sub)]
        off += sub
        ag = jnp.concatenate([_dot(n2[r:r + UP_DOT_ROWS, :], w_sub)
                              for r in range(0, tm, UP_DOT_ROWS)], axis=0)
        a = ag[:, :sub]
        gate = ag[:, sub:]
        w0 = cw_ref[0:1, cs]
        w1 = cw_ref[1:2, cs]
        w2 = cw_ref[2:3, cs]
        cb = cb_ref[:, cs]
        for r0 in range(0, tm, UP_ROWS):
            if r0 % sl:
                prev = a[r0 - hist:r0, :]
            elif nseg == 1:
                first = (i % tiles_per_seq) == 0
                prev = jnp.where(first, 0.0, carry_ref[:, cs])
            else:
                prev = st_ref[r0 // sl, :, cs]
            ac_rows = a[r0:r0 + UP_ROWS, :]
            win = jnp.concatenate([prev, ac_rows], axis=0)
            a_m1 = pltpu.roll(win, 1, 0)[hist:, :]
            a_m2 = pltpu.roll(win, 2, 0)[hist:, :]
            act = _silu(cb + w0 * a_m2 + w1 * a_m1 + w2 * ac_rows)
            m_ref[r0:r0 + UP_ROWS, cs] = (act * gate[r0:r0 + UP_ROWS, :]).astype(BF16)
        for s in range(nseg):
            tail = a[(s + 1) * sl - hist:(s + 1) * sl, :]
            if nseg == 1:
                carry_ref[:, cs] = tail
                cs_ref[0, :, cs] = tail
            else:
                cs_ref[s, :, cs] = tail


def _up(n2, wag, cw, cb, state8, *, nseg, tiles_per_seq):
    m = n2.shape[0]
    ncols = cw.shape[1]
    tm = 1024
    nt = m // tm
    if nseg == 1:
        cs_rows = nt // tiles_per_seq
        cs_spec = pl.BlockSpec((1, SUBLANES, ncols), lambda i: (i // tiles_per_seq, 0, 0))
        st_spec = pl.BlockSpec((1, SUBLANES, ncols), lambda i: (0, 0, 0))
    else:
        cs_rows = nt * nseg
        cs_spec = pl.BlockSpec((nseg, SUBLANES, ncols), lambda i: (i, 0, 0))
        st_spec = pl.BlockSpec((nseg, SUBLANES, ncols), lambda i: (i, 0, 0))
    return pl.pallas_call(
        functools.partial(_up_kernel, nseg=nseg, tiles_per_seq=tiles_per_seq),
        grid=(nt,),
        in_specs=[pl.BlockSpec((tm, D_MODEL), lambda i: (i, 0)),
                  pl.BlockSpec((D_MODEL, 2 * ncols), lambda i: (0, 0),
                               pipeline_mode=pl.Buffered(1)),
                  pl.BlockSpec((CONV_W, ncols), lambda i: (0, 0)),
                  pl.BlockSpec((1, ncols), lambda i: (0, 0)),
                  st_spec],
        out_specs=[pl.BlockSpec((tm, ncols), lambda i: (i, 0)), cs_spec],
        out_shape=[jax.ShapeDtypeStruct((m, ncols), BF16),
                   jax.ShapeDtypeStruct((cs_rows, SUBLANES, ncols), F32)],
        scratch_shapes=[pltpu.VMEM((SUBLANES, ncols), F32)],
        compiler_params=pltpu.CompilerParams(
            dimension_semantics=("arbitrary",), vmem_limit_bytes=VMEM_LIMIT_BYTES),
        name="up",
    )(n2, wag, cw, cb, state8)


def _down_kernel(*refs):
    *m_refs, w_ref, h_ref, y_ref = refs
    y = h_ref[...]
    for (col0, ncols), m_ref in zip(UP_RANGES, m_refs):
        y = y + _dot(m_ref[...], w_ref[col0:col0 + ncols, :])
    y_ref[...] = y


def _down(ms, w_down, h):
    m = h.shape[0]
    tm = 512
    return pl.pallas_call(
        _down_kernel,
        grid=(m // tm,),
        in_specs=[*[pl.BlockSpec((tm, nc), lambda i: (i, 0)) for _, nc in UP_RANGES],
                  pl.BlockSpec((D_FF_PAD, D_MODEL), lambda i: (0, 0),
                               pipeline_mode=pl.Buffered(1)),
                  pl.BlockSpec((tm, D_MODEL), lambda i: (i, 0))],
        out_specs=pl.BlockSpec((tm, D_MODEL), lambda i: (i, 0)),
        out_shape=jax.ShapeDtypeStruct((m, D_MODEL), F32),
        compiler_params=pltpu.CompilerParams(
            dimension_semantics=("arbitrary",), vmem_limit_bytes=VMEM_LIMIT_BYTES),
        name="down",
    )(*ms, w_down, h)


def _band_bias(table):
    d = jnp.arange(-(CHUNK - 1), BAND)
    e = table[:, jnp.clip(ATTN_WINDOW - d, -REL_CLIP, REL_CLIP) + REL_CLIP].astype(F32)
    n = e.shape[1]
    ep = jnp.pad(e, ((0, 0), (0, 1)))
    toep = jnp.tile(ep, (1, CHUNK))[:, :CHUNK * n].reshape(N_HEADS, CHUNK, n)
    return toep[:, :, CHUNK - 1:CHUNK - 1 + BAND] * LOG2E


def _group_bias(bias):
    pad = BAND_PAD - BAND
    even = jnp.pad(bias, ((0, 0), (0, 0), (0, pad)), constant_values=NEG_INF)
    odd = jnp.pad(bias, ((0, 0), (0, 0), (pad, 0)), constant_values=NEG_INF)
    return jnp.stack([even, odd], axis=1)


def _cast_up_rows(w_ref, o_refs):
    for (col0, ncols), o_ref in zip(UP_RANGES, o_refs):
        off = 0
        for width in _sub_widths(ncols):
            for half in range(2):
                src = half * D_FF + col0 + off
                dst = 2 * off + half * width
                o_ref[:, dst:dst + width] = w_ref[:, src:src + width].astype(BF16)
            off += width


def _cast_down_rows(w_ref, o_ref, block):
    tr = w_ref.shape[0]
    row = lax.broadcasted_iota(jnp.int32, w_ref.shape, 0) + block * tr
    o_ref[...] = jnp.where(row < D_FF, w_ref[...], 0.0).astype(BF16)


def _layer(x2d, seq, sample_cache, ffn_weights, nmg, w_in, qg, kg, bias, lng, lnb, ws, bs,
           w_out, nfg, cw, cb):
    m = x2d.shape[0]
    batch = m // seq
    is_sample = sample_cache is not None
    row = lambda v: v[None, :]
    if is_sample:
        wags, wd = ffn_weights
        z, kf, vf, vnf = _inproj(x2d, row(nmg), w_in, row(qg), row(kg), row(lng), row(lnb),
                                 keep_every=1, emit_vn=True)
        ck, cv, cst = sample_cache
        w_cache = ck.shape[1]
        oa = _attn_sample(z, ck.reshape(batch * w_cache * N_HEADS, HEAD_DIM),
                          cv.reshape(batch * w_cache * N_HEADS, HEAD_DIM),
                          bias[:, :, BAND - w_cache - seq:], batch=batch, t=seq)
        chunk = seq
        state8 = jnp.pad(cst, ((0, 0), (SUBLANES - (CONV_W - 1), 0), (0, 0)))
        nseg, tiles_per_seq = batch, 1
    else:
        w_up, w_down = ffn_weights
        keep_every = seq // 512
        z, kf, vf = _inproj(x2d, row(nmg), w_in, row(qg), row(kg), row(lng), row(lnb),
                            keep_every=keep_every, emit_vn=False)
        vnf = None
        oa, wags, wd = _attn_prompt(z, _group_bias(bias), w_up, w_down, batch=batch, seq=seq)
        chunk = GMLP_CHUNK
        state8 = jnp.zeros((1, SUBLANES, D_FF), F32)
        nseg, tiles_per_seq = 1, seq // 1024
    wsl = ws[:, :chunk, :chunk]
    bsb = jnp.broadcast_to(bs[:, :chunk, None], (N_GROUPS, chunk, GROUP_DIM))
    h, n2 = _outproj(oa, z, x2d, w_out, wsl, bsb, row(nfg), chunk=chunk)
    ms, tails = zip(*[
        _up(n2, wag, cw[:, c0:c0 + nc], cb[:, c0:c0 + nc], state8[:, :, c0:c0 + nc],
            nseg=nseg, tiles_per_seq=tiles_per_seq)
        for (c0, nc), wag in zip(UP_RANGES, wags)])
    y = _down(ms, wd, h)
    conv_state = jnp.concatenate(tails, axis=2)[:, SUBLANES - (CONV_W - 1):, :]
    return y, kf, vf, vnf, conv_state, (wags, wd)


def kernel(x_prompt, x_sample, cache_attn_k, cache_attn_v, state_ffn_conv, norm_mix_g, w_in,
           q_norm_g, k_norm_g, rel_bias_table, gmlp_ln_g, gmlp_ln_b, gmlp_w_s, gmlp_b_s, w_out,
           norm_ffn_g, w_up, ffn_conv_w, ffn_conv_b, w_down):
    batch, seq, _ = x_prompt.shape
    dbatch, dseq, _ = x_sample.shape
    depth = w_in.shape[0]
    xp = x_prompt.reshape(batch * seq, D_MODEL)
    xs = x_sample.reshape(dbatch * dseq, D_MODEL)
    keep = min(ATTN_WINDOW, seq)
    outs = [[] for _ in range(7)]
    for l in range(depth):
        shared = (norm_mix_g[l], w_in[l].astype(BF16), q_norm_g[l], k_norm_g[l],
                  _band_bias(rel_bias_table[l]), gmlp_ln_g[l], gmlp_ln_b[l], gmlp_w_s[l],
                  gmlp_b_s[l], w_out[l].astype(BF16), norm_ffn_g[l],
                  ffn_conv_w[l], ffn_conv_b[l][None, :])
        xp, kp, vp, _, cp, ffn_bf16 = _layer(xp, seq, None, (w_up[l], w_down[l]), *shared)
        xs, ks, vs, gs, cs, _ = _layer(
            xs, dseq, (cache_attn_k[l], cache_attn_v[l], state_ffn_conv[l]), ffn_bf16, *shared)
        outs[0].append(kp.reshape(batch, keep, N_HEADS, HEAD_DIM))
        outs[1].append(vp.reshape(batch, keep, N_HEADS, HEAD_DIM))
        outs[2].append(cp)
        outs[3].append(ks.reshape(dbatch, dseq, N_HEADS, HEAD_DIM))
        outs[4].append(vs.reshape(dbatch, dseq, N_HEADS, HEAD_DIM))
        outs[5].append(gs.reshape(dbatch, dseq, D_GMLP))
        outs[6].append(cs)
    return (xp.reshape(batch, seq, D_MODEL), xs.reshape(dbatch, dseq, D_MODEL),
            *[jnp.stack(o) for o in outs])
```

```python
import functools

import jax
import jax.numpy as jnp
from jax import lax
from jax.experimental import pallas as pl
from jax.experimental.pallas import tpu as pltpu

D_MODEL = 2048
CHUNK = 64
ATTN_WINDOW = 8 * CHUNK
BAND = ATTN_WINDOW + CHUNK
D_ATTN = D_MODEL // 2
HEAD_DIM = 128
N_HEADS = D_ATTN // HEAD_DIM
D_GMLP = D_MODEL - D_ATTN
N_GROUPS = 8
GROUP_DIM = D_GMLP // N_GROUPS
GMLP_CHUNK = 128
REL_CLIP = 128
D_FF = 5504
CONV_W = 3
D_IN = 3 * D_ATTN + 2 * D_GMLP
EPS = 1e-6
NEG_INF = -1e30
LOG2E = 1.4426950408889634

LANES = 128
SUBLANES = 8
BF16_ROWS = 16
VMEM_LIMIT_BYTES = 60 * 1024 * 1024

UP_SUB = 256
UP_RANGES = ((0, 11 * UP_SUB), (11 * UP_SUB, D_FF - 11 * UP_SUB))
D_FF_PAD = 22 * UP_SUB
UP_ROWS = 64
UP_DOT_ROWS = 256
OUT_ROWS = 256
IN_DOT_ROWS = 256
ATTN_GROUP = 4
ATTN_HEADS_PER_STEP = 2
GQ = ATTN_GROUP * CHUNK
GK = GQ + ATTN_WINDOW
BAND_PAD = BAND + CHUNK
assert BAND_PAD % LANES == 0 and LANES == 2 * CHUNK

F32 = jnp.float32
BF16 = jnp.bfloat16


def _sub_widths(ncols):
    full, rest = divmod(ncols, UP_SUB)
    return (UP_SUB,) * full + ((rest,) if rest else ())


def _dot(a, b):
    return jnp.dot(a, b, preferred_element_type=F32)


def _gelu(x):
    return 0.5 * x * (1.0 + lax.erf(x * (0.5 ** 0.5)))


def _inproj_kernel(x_ref, nmg_ref, w_ref, qg_ref, kg_ref, lng_ref, lnb_ref,
                   z_ref, kf_ref, vf_ref, *rest, emit_vn):
    x = x_ref[...]
    ms = jnp.mean(x * x, axis=-1, keepdims=True)
    n = (x * lax.rsqrt(ms + EPS) * nmg_ref[...]).astype(BF16)

    def head_rms(zh, g):
        r = lax.rsqrt(jnp.mean(zh * zh, axis=-1, keepdims=True) + EPS)
        return zh * r * g

    sub = 4 * HEAD_DIM
    order = sorted(range(D_IN // sub), key=lambda sb: (4, 3, 0, 1, 2).index(sb * sub // D_ATTN))
    for sb in order:
        seg = (sb * sub) // D_ATTN
        w_sub = w_ref[:, sb * sub:(sb + 1) * sub]
        zs = jnp.concatenate([_dot(n[r:r + IN_DOT_ROWS, :], w_sub)
                              for r in range(0, n.shape[0], IN_DOT_ROWS)], axis=0)
        for hh in range(sub // HEAD_DIM):
            c0 = sb * sub + hh * HEAD_DIM
            lo = c0 - seg * D_ATTN
            zh = zs[:, hh * HEAD_DIM:(hh + 1) * HEAD_DIM]
            if seg == 0:
                out = head_rms(zh, qg_ref[...] * (HEAD_DIM ** -0.5 * LOG2E))
            elif seg == 1:
                out = head_rms(zh, kg_ref[...])
            elif seg == 2:
                out = zh
            elif seg == 3:
                out = _gelu(zh)
            else:
                ge = _gelu(zh)
                mu = jnp.mean(ge, axis=-1, keepdims=True)
                d = ge - mu
                var = jnp.mean(d * d, axis=-1, keepdims=True)
                out = d * lax.rsqrt(var + EPS) * lng_ref[...] + lnb_ref[...]
            z_ref[:, c0:c0 + HEAD_DIM] = out.astype(BF16)
            head_rows = pl.ds(lo // HEAD_DIM, zh.shape[0], stride=N_HEADS)
            if seg == 1:
                kf_ref[head_rows, :] = out
            elif seg == 2:
                vf_ref[head_rows, :] = out
            elif seg == 4 and emit_vn:
                rest[0][:, lo:lo + HEAD_DIM] = out


def _inproj(x2d, nmg, w_in, qg, kg, lng, lnb, *, keep_every, emit_vn):
    m = x2d.shape[0]
    tm = 512
    nt = m // tm
    nkeep = nt // keep_every
    vec = lambda width: pl.BlockSpec((1, width), lambda i: (0, 0))
    keep_spec = pl.BlockSpec((tm * N_HEADS, HEAD_DIM), lambda i: (i // keep_every, 0))
    out_specs = [pl.BlockSpec((tm, D_IN), lambda i: (i, 0)), keep_spec, keep_spec]
    out_shape = [jax.ShapeDtypeStruct((m, D_IN), BF16),
                 jax.ShapeDtypeStruct((nkeep * tm * N_HEADS, HEAD_DIM), F32),
                 jax.ShapeDtypeStruct((nkeep * tm * N_HEADS, HEAD_DIM), F32)]
    if emit_vn:
        out_specs.append(pl.BlockSpec((tm, D_GMLP), lambda i: (i, 0)))
        out_shape.append(jax.ShapeDtypeStruct((m, D_GMLP), F32))
    return pl.pallas_call(
        functools.partial(_inproj_kernel, emit_vn=emit_vn),
        grid=(nt,),
        in_specs=[pl.BlockSpec((tm, D_MODEL), lambda i: (i, 0)),
                  vec(D_MODEL),
                  pl.BlockSpec((D_MODEL, D_IN), lambda i: (0, 0),
                               pipeline_mode=pl.Buffered(1)),
                  vec(HEAD_DIM), vec(HEAD_DIM), vec(GROUP_DIM), vec(GROUP_DIM)],
        out_specs=out_specs,
        out_shape=out_shape,
        compiler_params=pltpu.CompilerParams(
            dimension_semantics=("arbitrary",), vmem_limit_bytes=VMEM_LIMIT_BYTES),
        name="inproj",
    )(x2d, nmg, w_in, qg, kg, lng, lnb)


def _softmax_numer(sc):
    m = jnp.max(sc, axis=-1, keepdims=True)
    return jnp.exp2(sc - m).astype(BF16)


def _pv(p, v):
    ol = _dot(p, v)
    return ol[:, :HEAD_DIM] * (1.0 / ol[:, HEAD_DIM:])


def _qk(q, k):
    return lax.dot_general(q, k, (((1,), (1,)), ((), ())), preferred_element_type=F32)


def _attn_prompt_kernel(q_ref, k_ref, v_ref, bm_ref, wup_ref, wdn_ref, o_ref, *rest):
    *wag_refs, wd_ref, kp_ref, vp_ref = rest
    step = pl.program_id(0) * pl.num_programs(1) + pl.program_id(1)
    _cast_up_rows(wup_ref, wag_refs)
    _cast_down_rows(wdn_ref, wd_ref, step)
    seq = q_ref.shape[0]
    heads = [slice(hd * HEAD_DIM, (hd + 1) * HEAD_DIM) for hd in range(ATTN_HEADS_PER_STEP)]
    zeros = jnp.zeros((ATTN_WINDOW, HEAD_DIM), BF16)
    for hd, hc in enumerate(heads):
        kp_ref[hd, 0:ATTN_WINDOW, :] = zeros
        vp_ref[hd, 0:ATTN_WINDOW, 0:HEAD_DIM] = zeros
        kp_ref[hd, ATTN_WINDOW:, :] = k_ref[:, hc]
        vp_ref[hd, ATTN_WINDOW:, 0:HEAD_DIM] = v_ref[:, hc]
        vp_ref[hd, :, HEAD_DIM:] = jnp.ones((seq + ATTN_WINDOW, HEAD_DIM), BF16)

    def group(hd, q0, pre_stream):
        q = q_ref[pl.ds(q0, GQ), heads[hd]]
        kb = kp_ref[hd, pl.ds(q0, GK), :]
        vb = vp_ref[hd, pl.ds(q0, GK), :]
        s = _qk(q, kb)
        rows = []
        for c in range(ATTN_GROUP):
            off = (c * CHUNK) // LANES * LANES
            variant = (c * CHUNK - off) // CHUNK
            sc = s[c * CHUNK:(c + 1) * CHUNK, off:off + BAND_PAD] + bm_ref[hd, variant]
            if pre_stream:
                col = lax.broadcasted_iota(jnp.int32, (CHUNK, BAND_PAD), 1) + off
                sc = jnp.where(col >= ATTN_WINDOW - q0, sc, NEG_INF)
            parts = [_softmax_numer(sc)]
            if off:
                parts.insert(0, jnp.zeros((CHUNK, off), BF16))
            if GK - off - BAND_PAD:
                parts.append(jnp.zeros((CHUNK, GK - off - BAND_PAD), BF16))
            rows.append(jnp.concatenate(parts, axis=1))
        p = jnp.concatenate(rows, axis=0)
        o_ref[pl.ds(q0, GQ), heads[hd]] = _pv(p, vb).astype(BF16)

    n_pre = ATTN_WINDOW // GQ
    for gi in range(n_pre):
        for hd in range(ATTN_HEADS_PER_STEP):
            group(hd, gi * GQ, True)

    def body(gi, carry):
        for hd in range(ATTN_HEADS_PER_STEP):
            group(hd, pl.multiple_of(gi * GQ, GQ), False)
        return carry

    lax.fori_loop(n_pre, seq // GQ, body, 0, unroll=True)


def _attn_prompt(z, bm, w_up, w_down, *, batch, seq):
    hp = ATTN_HEADS_PER_STEP
    nh = N_HEADS // hp
    steps = batch * nh
    up_rows = D_MODEL // steps
    down_rows = -(-D_FF_PAD // (steps * BF16_ROWS)) * BF16_ROWS
    assert up_rows * steps == D_MODEL and up_rows % BF16_ROWS == 0
    last_down = (D_FF - 1) // down_rows
    step = lambda b, h: b * nh + h
    row_block = lambda width: pl.BlockSpec((up_rows, width), lambda b, h: (step(b, h), 0))
    head_block = lambda seg: pl.BlockSpec((seq, hp * HEAD_DIM), lambda b, h: (b, seg * nh + h))
    oa, *wags, wd = pl.pallas_call(
        _attn_prompt_kernel,
        grid=(batch, nh),
        in_specs=[head_block(0), head_block(1), head_block(2),
                  pl.BlockSpec((hp, 2, CHUNK, BAND_PAD), lambda b, h: (h, 0, 0, 0)),
                  row_block(2 * D_FF),
                  pl.BlockSpec((down_rows, D_MODEL),
                               lambda b, h: (jnp.minimum(step(b, h), last_down), 0))],
        out_specs=[head_block(0),
                   *[row_block(2 * nc) for _, nc in UP_RANGES],
                   pl.BlockSpec((down_rows, D_MODEL), lambda b, h: (step(b, h), 0))],
        out_shape=[jax.ShapeDtypeStruct((batch * seq, D_ATTN), BF16),
                   *[jax.ShapeDtypeStruct((D_MODEL, 2 * nc), BF16) for _, nc in UP_RANGES],
                   jax.ShapeDtypeStruct((steps * down_rows, D_MODEL), BF16)],
        scratch_shapes=[pltpu.VMEM((hp, seq + ATTN_WINDOW, HEAD_DIM), BF16),
                        pltpu.VMEM((hp, seq + ATTN_WINDOW, 2 * HEAD_DIM), BF16)],
        compiler_params=pltpu.CompilerParams(
            dimension_semantics=("arbitrary", "arbitrary"),
            vmem_limit_bytes=VMEM_LIMIT_BYTES),
        name="attn_prompt",
    )(z, z, z, bm, w_up, w_down)
    return oa, wags, wd


def _attn_sample_kernel(q_ref, kn_ref, vn_ref, kc_ref, vc_ref, bias_ref, o_ref):
    w = kc_ref.shape[0] // N_HEADS
    ones = jnp.ones((w + kn_ref.shape[0], HEAD_DIM), BF16)
    for h in range(N_HEADS):
        hc = slice(h * HEAD_DIM, (h + 1) * HEAD_DIM)
        head_rows = pl.ds(h, w, stride=N_HEADS)
        q = q_ref[:, hc]
        s = jnp.concatenate([_qk(q, kc_ref[head_rows, :].astype(BF16)),
                             _qk(q, kn_ref[:, hc])], axis=1)
        v = jnp.concatenate([vc_ref[head_rows, :].astype(BF16), vn_ref[:, hc]], axis=0)
        o = _pv(_softmax_numer(s + bias_ref[h]), jnp.concatenate([v, ones], axis=1))
        o_ref[:, hc] = o.astype(BF16)


def _attn_sample(z, k_cache, v_cache, bias, *, batch, t):
    w = k_cache.shape[0] // (batch * N_HEADS)
    return pl.pallas_call(
        _attn_sample_kernel,
        grid=(batch,),
        in_specs=[pl.BlockSpec((t, D_ATTN), lambda b: (b, 0)),
                  pl.BlockSpec((t, D_ATTN), lambda b: (b, 1)),
                  pl.BlockSpec((t, D_ATTN), lambda b: (b, 2)),
                  pl.BlockSpec((w * N_HEADS, HEAD_DIM), lambda b: (b, 0)),
                  pl.BlockSpec((w * N_HEADS, HEAD_DIM), lambda b: (b, 0)),
                  pl.BlockSpec((N_HEADS, t, w + t), lambda b: (0, 0, 0))],
        out_specs=pl.BlockSpec((t, D_ATTN), lambda b: (b, 0)),
        out_shape=jax.ShapeDtypeStruct((batch * t, D_ATTN), BF16),
        compiler_params=pltpu.CompilerParams(
            dimension_semantics=("arbitrary",), vmem_limit_bytes=VMEM_LIMIT_BYTES),
        name="attn_sample",
    )(z, z, z, k_cache, v_cache, bias)


def _outproj_kernel(oa_ref, u_ref, vn_ref, x_ref, wout_ref, ws_ref, bsb_ref, g_ref,
                    h_ref, n2_ref, *, chunk):
    tm = x_ref.shape[0]
    row = lax.broadcasted_iota(jnp.int32, (chunk, chunk), 0)
    col = lax.broadcasted_iota(jnp.int32, (chunk, chunk), 1)
    tri = row >= col
    wsg = [jnp.where(tri, ws_ref[g], 0.0).astype(BF16) for g in range(N_GROUPS)]
    nck = OUT_ROWS // chunk
    for r0 in range(0, tm, OUT_ROWS):
        ob = [[None] * N_GROUPS for _ in range(nck)]
        for g in range(N_GROUPS):
            gc = slice(g * GROUP_DIM, (g + 1) * GROUP_DIM)
            vn = jnp.concatenate([vn_ref[r0 + c * chunk:r0 + (c + 1) * chunk, gc]
                                  for c in range(nck)], axis=1)
            vs = _dot(wsg[g], vn)
            for c in range(nck):
                rows = slice(r0 + c * chunk, r0 + (c + 1) * chunk)
                gate = vs[:, c * GROUP_DIM:(c + 1) * GROUP_DIM] + bsb_ref[g]
                ob[c][g] = (u_ref[rows, gc].astype(F32) * gate).astype(BF16)
        ob = jnp.concatenate([jnp.concatenate(obc, axis=1) for obc in ob], axis=0)
        rows = slice(r0, r0 + OUT_ROWS)
        h = (x_ref[rows, :] + _dot(oa_ref[rows, :], wout_ref[0:D_ATTN, :])
             + _dot(ob, wout_ref[D_ATTN:, :]))
        h_ref[rows, :] = h
        ms = jnp.mean(h * h, axis=-1, keepdims=True)
        n2_ref[rows, :] = (h * lax.rsqrt(ms + EPS) * g_ref[...]).astype(BF16)


def _outproj(oa, z, x2d, w_out, ws, bsb, nfg, *, chunk):
    m = x2d.shape[0]
    tm = 512
    return pl.pallas_call(
        functools.partial(_outproj_kernel, chunk=chunk),
        grid=(m // tm,),
        in_specs=[pl.BlockSpec((tm, D_ATTN), lambda i: (i, 0)),
                  pl.BlockSpec((tm, D_GMLP), lambda i: (i, 3)),
                  pl.BlockSpec((tm, D_GMLP), lambda i: (i, 4)),
                  pl.BlockSpec((tm, D_MODEL), lambda i: (i, 0)),
                  pl.BlockSpec((D_MODEL, D_MODEL), lambda i: (0, 0),
                               pipeline_mode=pl.Buffered(1)),
                  pl.BlockSpec((N_GROUPS, chunk, chunk), lambda i: (0, 0, 0)),
                  pl.BlockSpec((N_GROUPS, chunk, GROUP_DIM), lambda i: (0, 0, 0)),
                  pl.BlockSpec((1, D_MODEL), lambda i: (0, 0))],
        out_specs=[pl.BlockSpec((tm, D_MODEL), lambda i: (i, 0)),
                   pl.BlockSpec((tm, D_MODEL), lambda i: (i, 0))],
        out_shape=[jax.ShapeDtypeStruct((m, D_MODEL), F32),
                   jax.ShapeDtypeStruct((m, D_MODEL), BF16)],
        compiler_params=pltpu.CompilerParams(
            dimension_semantics=("arbitrary",), vmem_limit_bytes=VMEM_LIMIT_BYTES),
        name="outproj",
    )(oa, z, z, x2d, w_out, ws, bsb, nfg)


def _silu(x):
    return x * (1.0 / (1.0 + jnp.exp(-x)))


def _up_kernel(n2_ref, w_ref, cw_ref, cb_ref, st_ref, m_ref, cs_ref,
               carry_ref, *, nseg, tiles_per_seq):
    i = pl.program_id(0)
    tm = n2_ref.shape[0]
    sl = tm // nseg
    hist = SUBLANES
    n2 = n2_ref[...]
    off = 0
    for sub in _sub_widths(m_ref.shape[1]):
        cs = slice(off, off + sub)
        w_sub = w_ref[:, 2 * off:2 * (off + sub)]
        off += sub
        ag = jnp.concatenate([_dot(n2[r:r + UP_DOT_ROWS, :], w_sub)
                              for r in range(0, tm, UP_DOT_ROWS)], axis=0)
        a = ag[:, :sub]
        gate = ag[:, sub:]
        w0 = cw_ref[0:1, cs]
        w1 = cw_ref[1:2, cs]
        w2 = cw_ref[2:3, cs]
        cb = cb_ref[:, cs]
        for r0 in range(0, tm, UP_ROWS):
            if r0 % sl:
                prev = a[r0 - hist:r0, :]
            elif nseg == 1:
                first = (i % tiles_per_seq) == 0
                prev = jnp.where(first, 0.0, carry_ref[:, cs])
            else:
                prev = st_ref[r0 // sl, :, cs]
            ac_rows = a[r0:r0 + UP_ROWS, :]
            win = jnp.concatenate([prev, ac_rows], axis=0)
            a_m1 = pltpu.roll(win, 1, 0)[hist:, :]
            a_m2 = pltpu.roll(win, 2, 0)[hist:, :]
            act = _silu(cb + w0 * a_m2 + w1 * a_m1 + w2 * ac_rows)
            m_ref[r0:r0 + UP_ROWS, cs] = (act * gate[r0:r0 + UP_ROWS, :]).astype(BF16)
        for s in range(nseg):
            tail = a[(s + 1) * sl - hist:(s + 1) * sl, :]
            if nseg == 1:
                carry_ref[:, cs] = tail
                cs_ref[0, :, cs] = tail
            else:
                cs_ref[s, :, cs] = tail


def _up(n2, wag, cw, cb, state8, *, nseg, tiles_per_seq):
    m = n2.shape[0]
    ncols = cw.shape[1]
    tm = 1024
    nt = m // tm
    if nseg == 1:
        cs_rows = nt // tiles_per_seq
        cs_spec = pl.BlockSpec((1, SUBLANES, ncols), lambda i: (i // tiles_per_seq, 0, 0))
        st_spec = pl.BlockSpec((1, SUBLANES, ncols), lambda i: (0, 0, 0))
    else:
        cs_rows = nt * nseg
        cs_spec = pl.BlockSpec((nseg, SUBLANES, ncols), lambda i: (i, 0, 0))
        st_spec = pl.BlockSpec((nseg, SUBLANES, ncols), lambda i: (i, 0, 0))
    return pl.pallas_call(
        functools.partial(_up_kernel, nseg=nseg, tiles_per_seq=tiles_per_seq),
        grid=(nt,),
        in_specs=[pl.BlockSpec((tm, D_MODEL), lambda i: (i, 0)),
                  pl.BlockSpec((D_MODEL, 2 * ncols), lambda i: (0, 0),
                               pipeline_mode=pl.Buffered(1)),
                  pl.BlockSpec((CONV_W, ncols), lambda i: (0, 0)),
                  pl.BlockSpec((1, ncols), lambda i: (0, 0)),
                  st_spec],
        out_specs=[pl.BlockSpec((tm, ncols), lambda i: (i, 0)), cs_spec],
        out_shape=[jax.ShapeDtypeStruct((m, ncols), BF16),
                   jax.ShapeDtypeStruct((cs_rows, SUBLANES, ncols), F32)],
        scratch_shapes=[pltpu.VMEM((SUBLANES, ncols), F32)],
        compiler_params=pltpu.CompilerParams(
            dimension_semantics=("arbitrary",), vmem_limit_bytes=VMEM_LIMIT_BYTES),
        name="up",
    )(n2, wag, cw, cb, state8)


def _down_kernel(*refs):
    *m_refs, w_ref, h_ref, y_ref = refs
    y = h_ref[...]
    for (col0, ncols), m_ref in zip(UP_RANGES, m_refs):
        y = y + _dot(m_ref[...], w_ref[col0:col0 + ncols, :])
    y_ref[...] = y


def _down(ms, w_down, h):
    m = h.shape[0]
    tm = 512
    return pl.pallas_call(
        _down_kernel,
        grid=(m // tm,),
        in_specs=[*[pl.BlockSpec((tm, nc), lambda i: (i, 0)) for _, nc in UP_RANGES],
                  pl.BlockSpec((D_FF_PAD, D_MODEL), lambda i: (0, 0),
                               pipeline_mode=pl.Buffered(1)),
                  pl.BlockSpec((tm, D_MODEL), lambda i: (i, 0))],
        out_specs=pl.BlockSpec((tm, D_MODEL), lambda i: (i, 0)),
        out_shape=jax.ShapeDtypeStruct((m, D_MODEL), F32),
        compiler_params=pltpu.CompilerParams(
            dimension_semantics=("arbitrary",), vmem_limit_bytes=VMEM_LIMIT_BYTES),
        name="down",
    )(*ms, w_down, h)


def _band_bias(table):
    d = jnp.arange(-(CHUNK - 1), BAND)
    e = table[:, jnp.clip(ATTN_WINDOW - d, -REL_CLIP, REL_CLIP) + REL_CLIP].astype(F32)
    n = e.shape[1]
    ep = jnp.pad(e, ((0, 0), (0, 1)))
    toep = jnp.tile(ep, (1, CHUNK))[:, :CHUNK * n].reshape(N_HEADS, CHUNK, n)
    return toep[:, :, CHUNK - 1:CHUNK - 1 + BAND] * LOG2E


def _group_bias(bias):
    pad = BAND_PAD - BAND
    even = jnp.pad(bias, ((0, 0), (0, 0), (0, pad)), constant_values=NEG_INF)
    odd = jnp.pad(bias, ((0, 0), (0, 0), (pad, 0)), constant_values=NEG_INF)
    return jnp.stack([even, odd], axis=1)


def _cast_up_rows(w_ref, o_refs):
    for (col0, ncols), o_ref in zip(UP_RANGES, o_refs):
        off = 0
        for width in _sub_widths(ncols):
            for half in range(2):
                src = half * D_FF + col0 + off
                dst = 2 * off + half * width
                o_ref[:, dst:dst + width] = w_ref[:, src:src + width].astype(BF16)
            off += width


def _cast_down_rows(w_ref, o_ref, block):
    tr = w_ref.shape[0]
    row = lax.broadcasted_iota(jnp.int32, w_ref.shape, 0) + block * tr
    o_ref[...] = jnp.where(row < D_FF, w_ref[...], 0.0).astype(BF16)


def _layer(x2d, seq, sample_cache, ffn_weights, nmg, w_in, qg, kg, bias, lng, lnb, ws, bs,
           w_out, nfg, cw, cb):
    m = x2d.shape[0]
    batch = m // seq
    is_sample = sample_cache is not None
    row = lambda v: v[None, :]
    if is_sample:
        wags, wd = ffn_weights
        z, kf, vf, vnf = _inproj(x2d, row(nmg), w_in, row(qg), row(kg), row(lng), row(lnb),
                                 keep_every=1, emit_vn=True)
        ck, cv, cst = sample_cache
        w_cache = ck.shape[1]
        oa = _attn_sample(z, ck.reshape(batch * w_cache * N_HEADS, HEAD_DIM),
                          cv.reshape(batch * w_cache * N_HEADS, HEAD_DIM),
                          bias[:, :, BAND - w_cache - seq:], batch=batch, t=seq)
        chunk = seq
        state8 = jnp.pad(cst, ((0, 0), (SUBLANES - (CONV_W - 1), 0), (0, 0)))
        nseg, tiles_per_seq = batch, 1
    else:
        w_up, w_down = ffn_weights
        keep_every = seq // 512
        z, kf, vf = _inproj(x2d, row(nmg), w_in, row(qg), row(kg), row(lng), row(lnb),
                            keep_every=keep_every, emit_vn=False)
        vnf = None
        oa, wags, wd = _attn_prompt(z, _group_bias(bias), w_up, w_down, batch=batch, seq=seq)
        chunk = GMLP_CHUNK
        state8 = jnp.zeros((1, SUBLANES, D_FF), F32)
        nseg, tiles_per_seq = 1, seq // 1024
    wsl = ws[:, :chunk, :chunk]
    bsb = jnp.broadcast_to(bs[:, :chunk, None], (N_GROUPS, chunk, GROUP_DIM))
    h, n2 = _outproj(oa, z, x2d, w_out, wsl, bsb, row(nfg), chunk=chunk)
    ms, tails = zip(*[
        _up(n2, wag, cw[:, c0:c0 + nc], cb[:, c0:c0 + nc], state8[:, :, c0:c0 + nc],
            nseg=nseg, tiles_per_seq=tiles_per_seq)
        for (c0, nc), wag in zip(UP_RANGES, wags)])
    y = _down(ms, wd, h)
    conv_state = jnp.concatenate(tails, axis=2)[:, SUBLANES - (CONV_W - 1):, :]
    return y, kf, vf, vnf, conv_state, (wags, wd)


def kernel(x_prompt, x_sample, cache_attn_k, cache_attn_v, state_ffn_conv, norm_mix_g, w_in,
           q_norm_g, k_norm_g, rel_bias_table, gmlp_ln_g, gmlp_ln_b, gmlp_w_s, gmlp_b_s, w_out,
           norm_ffn_g, w_up, ffn_conv_w, ffn_conv_b, w_down):
    batch, seq, _ = x_prompt.shape
    dbatch, dseq, _ = x_sample.shape
    depth = w_in.shape[0]
    xp = x_prompt.reshape(batch * seq, D_MODEL)
    xs = x_sample.reshape(dbatch * dseq, D_MODEL)
    keep = min(ATTN_WINDOW, seq)
    outs = [[] for _ in range(7)]
    for l in range(depth):
        shared = (norm_mix_g[l], w_in[l].astype(BF16), q_norm_g[l], k_norm_g[l],
                  _band_bias(rel_bias_table[l]), gmlp_ln_g[l], gmlp_ln_b[l], gmlp_w_s[l],
                  gmlp_b_s[l], w_out[l].astype(BF16), norm_ffn_g[l],
                  ffn_conv_w[l], ffn_conv_b[l][None, :])
        xp, kp, vp, _, cp, ffn_bf16 = _layer(xp, seq, None, (w_up[l], w_down[l]), *shared)
        xs, ks, vs, gs, cs, _ = _layer(
            xs, dseq, (cache_attn_k[l], cache_attn_v[l], state_ffn_conv[l]), ffn_bf16, *shared)
        outs[0].append(kp.reshape(batch, keep, N_HEADS, HEAD_DIM))
        outs[1].append(vp.reshape(batch, keep, N_HEADS, HEAD_DIM))
        outs[2].append(cp)
        outs[3].append(ks.reshape(dbatch, dseq, N_HEADS, HEAD_DIM))
        outs[4].append(vs.reshape(dbatch, dseq, N_HEADS, HEAD_DIM))
        outs[5].append(gs.reshape(dbatch, dseq, D_GMLP))
        outs[6].append(cs)
    return (xp.reshape(batch, seq, D_MODEL), xs.reshape(dbatch, dseq, D_MODEL),
            *[jnp.stack(o) for o in outs])
```
